```python
import functools
import jax, jax.numpy as jnp
from jax import lax
import numpy as np

D_MODEL = 1024
BATCH = 8
SEQ = 2048
DEPTH = 2
DEC_BATCH = 32
DEC_SEQ = 1
PAST_LEN = 8192
PAGE_SIZE = 128

BRANCH_W = 512
HA = 8
DA = BRANCH_W // HA
WA = HA * DA
WB = BRANCH_W
NBLK_B = 8
BB = WB // NBLK_B
CONV_B = 4
LRU_C = 8.0
WC = BRANCH_W
CONV_C = 31
N_BRANCH = 3
D_IN = 3 * WA + 2 * WB + 2 * WC + N_BRANCH * D_MODEL
N_EXPERTS = 32
TOP_K = 4
D_FF = D_MODEL
SWIGLU_LIMIT = 7.0
SWIGLU_ALPHA = 1.702
Q_BLOCK = 128
DN_ALPHA = (2 * DEPTH) ** 0.25
DN_BETA = (8 * DEPTH) ** -0.25
LN_EPS = 1e-5
SB_BIAS_HI = -5.0
SB_BIAS_LO = -9.0

kernel_name = "hybrid_sb_rglru_conformer_moe_step"


def layer_norm(x, g, b):
    xf = x.astype(jnp.float32)
    mu = jnp.mean(xf, axis=-1, keepdims=True)
    var = jnp.mean(jnp.square(xf - mu), axis=-1, keepdims=True)
    return ((xf - mu) * lax.rsqrt(var + LN_EPS)).astype(x.dtype) * g + b


def sb_block(q, k, v, bias, q_pos, k_pos):
    z = jnp.einsum("bqhd,bkhd->bhqk", q, k).astype(jnp.float32) * (DA ** -0.5) \
        + bias.astype(jnp.float32)[None, :, None, None]
    causal = k_pos[None, :] < q_pos[:, None]
    log_rest = jnp.where(causal, jax.nn.log_sigmoid(-z), 0.0)
    after = lax.cumsum(log_rest, axis=3, reverse=True) - log_rest
    w = jnp.where(causal, jnp.exp(jax.nn.log_sigmoid(z) + after), 0.0)
    return jnp.einsum("bhqk,bkhd->bqhd", w.astype(v.dtype), v)


def sb_prompt(q, k, v, bias):
    b, s = q.shape[0], q.shape[1]
    nb = s // Q_BLOCK
    qb = q.reshape(b, nb, Q_BLOCK, HA, DA).transpose(1, 0, 2, 3, 4)
    k_pos = jnp.arange(s)

    def one_block(args):
        i, qi = args
        q_pos = i * Q_BLOCK + jnp.arange(Q_BLOCK)
        return sb_block(qi, k, v, bias, q_pos, k_pos)

    o = lax.map(one_block, (jnp.arange(nb), qb))
    return o.transpose(1, 0, 2, 3, 4).reshape(b, s, HA, DA)


def sb_sample(q, k, v, bias, k_past, v_past):
    past = k_past.shape[1]
    t = q.shape[1]
    k_all = jnp.concatenate([k_past.astype(k.dtype), k], axis=1)
    v_all = jnp.concatenate([v_past.astype(v.dtype), v], axis=1)
    return sb_block(q, k_all, v_all, bias, past + jnp.arange(t), jnp.arange(past + t))


def gather_pages(pool, page_table):
    g = pool[page_table]
    return g.reshape(g.shape[0], g.shape[1] * g.shape[2], g.shape[3], g.shape[4])


def causal_depthwise_conv(x, buf, w, b):
    xp = jnp.concatenate([buf.astype(x.dtype), x], axis=1)
    y = lax.conv_general_dilated(xp, w[:, None, :].astype(x.dtype), window_strides=(1,), padding="VALID",
                                 dimension_numbers=("NWC", "WIO", "NWC"), feature_group_count=x.shape[-1])
    return y + b, xp[:, xp.shape[1] - (w.shape[0] - 1):]


def rg_lru(x, h0, wa, ba, wx, bx, lam):
    b, t, _ = x.shape
    xb = x.reshape(b, t, NBLK_B, BB)
    r = jax.nn.sigmoid(jnp.einsum("btni,nij->btnj", xb, wa).reshape(b, t, WB) + ba)
    i = jax.nn.sigmoid(jnp.einsum("btni,nij->btnj", xb, wx).reshape(b, t, WB) + bx)
    log_a = LRU_C * r.astype(jnp.float32) * jax.nn.log_sigmoid(lam.astype(jnp.float32))
    a = jnp.exp(log_a)
    u = jnp.sqrt(-jnp.expm1(2.0 * log_a)) * (i * x).astype(jnp.float32)

    def step(h, au):
        a_t, u_t = au
        h = a_t * h + u_t
        return h, h

    h_last, hs = lax.scan(step, h0.astype(jnp.float32), (a.transpose(1, 0, 2), u.transpose(1, 0, 2)))
    return hs.transpose(1, 0, 2).astype(x.dtype), h_last.astype(x.dtype)


def moe_block_size(n_assign):
    blk = 128
    while blk > 8 and blk * N_EXPERTS > n_assign:
        blk //= 2
    return blk


def moe(x, router_w, router_b, w_gate, b_gate, w_up, b_up, w_down, b_down):
    b, t, d = x.shape
    xt = x.reshape(b * t, d)
    n_tok = xt.shape[0]
    logits = (xt @ router_w + router_b).astype(jnp.float32)
    top_val, top_idx = lax.top_k(logits, TOP_K)
    gate_w = jax.nn.softmax(top_val, axis=-1).astype(x.dtype)
    n_assign = n_tok * TOP_K
    blk = moe_block_size(n_assign)
    flat_e = top_idx.reshape(-1)
    order = jnp.argsort(flat_e)
    sorted_e = flat_e[order]
    sorted_tok = (order // TOP_K).astype(jnp.int32)
    counts = jnp.bincount(flat_e, length=N_EXPERTS)
    padded = (counts + blk - 1) // blk * blk
    starts = jnp.cumsum(counts) - counts
    ends_p = jnp.cumsum(padded)
    pstarts = ends_p - padded
    dest = pstarts[sorted_e] + (jnp.arange(n_assign) - starts[sorted_e])
    n_blocks = -(-n_assign // blk) + N_EXPERTS
    tok_buf = jnp.full((n_blocks * blk,), n_tok, jnp.int32).at[dest].set(sorted_tok)
    block_e = jnp.clip(jnp.searchsorted(ends_p, jnp.arange(n_blocks) * blk, side="right"), 0, N_EXPERTS - 1)
    x_pad = jnp.concatenate([xt, jnp.zeros((1, d), xt.dtype)], axis=0)
    xb = x_pad[tok_buf].reshape(n_blocks, blk, d)

    def expert_block(args):
        xi, e = args
        g = jnp.minimum(xi @ w_gate[e] + b_gate[e], SWIGLU_LIMIT)
        u = jnp.clip(xi @ w_up[e] + b_up[e], -SWIGLU_LIMIT, SWIGLU_LIMIT)
        h = (u + 1.0) * (g * jax.nn.sigmoid(SWIGLU_ALPHA * g))
        return h @ w_down[e] + b_down[e]

    yb = lax.map(expert_block, (xb, block_e)).reshape(n_blocks * blk, d)
    y_sorted = yb[dest]
    y_assign = jnp.zeros_like(y_sorted).at[order].set(y_sorted).reshape(n_tok, TOP_K, d)
    return jnp.einsum("tk,tkd->td", gate_w, y_assign).reshape(b, t, d)


def trunk_layer(x, attend, h0, lru_buf0, conf_buf0,
                w_in, b_merge, sb_bias, lru_conv_w, lru_conv_b, lru_wa, lru_ba, lru_wx, lru_bx, lru_lambda,
                conf_conv_w, conf_conv_b, conf_ln_g, conf_ln_b, w_branch, w_out, ln1_g, ln1_b,
                router_w, router_b, w_gate, b_gate, w_up, b_up, w_down, b_down, ln2_g, ln2_b):
    b, t, _ = x.shape
    proj = x @ w_in
    splits = [WA, 2 * WA, 3 * WA, 3 * WA + WB, 3 * WA + 2 * WB, 3 * WA + 2 * WB + WC, 3 * WA + 2 * WB + 2 * WC]
    q, k, v, lru_x, lru_g, glu_a, glu_b, gates = jnp.split(proj, splits, axis=-1)
    q = q.reshape(b, t, HA, DA)
    k = k.reshape(b, t, HA, DA)
    v = v.reshape(b, t, HA, DA)
    o_a = attend(q, k, v, sb_bias).reshape(b, t, WA)
    xc, lru_buf = causal_depthwise_conv(lru_x, lru_buf0, lru_conv_w, lru_conv_b)
    hs, h_last = rg_lru(xc, h0, lru_wa, lru_ba, lru_wx, lru_bx, lru_lambda)
    o_b = hs * jax.nn.gelu(lru_g)
    glu = glu_a * jax.nn.sigmoid(glu_b)
    cc, conf_buf = causal_depthwise_conv(glu, conf_buf0, conf_conv_w, conf_conv_b)
    o_c = jax.nn.silu(layer_norm(cc, conf_ln_g, conf_ln_b))
    branches = jnp.stack([o_a, o_b, o_c], axis=2)
    bproj = jnp.einsum("btnw,nwd->btnd", branches, w_branch)
    g = jax.nn.sigmoid(gates.reshape(b, t, N_BRANCH, D_MODEL) + b_merge)
    merged = jnp.sum(g * bproj, axis=2)
    x = layer_norm(DN_ALPHA * x + merged @ w_out, ln1_g, ln1_b)
    x = layer_norm(DN_ALPHA * x + moe(x, router_w, router_b, w_gate, b_gate, w_up, b_up, w_down, b_down), ln2_g, ln2_b)
    return x, k, v, h_last, lru_buf, conf_buf


def _normal(key, shape, scale):
    return jax.random.normal(key, shape, jnp.float32) * scale


def setup_inputs(seed: int = 0) -> dict:
    key = jax.random.key(seed)
    ks = jax.random.split(key, 40)
    n_pages = PAST_LEN // PAGE_SIZE
    n_used = DEC_BATCH * n_pages
    n_phys = n_used + (n_used + 3) // 4
    page_table = jax.random.permutation(ks[0], n_phys)[:n_used].reshape(DEC_BATCH, n_pages).astype(jnp.int32)
    u = jax.random.uniform(ks[13], (DEPTH, WB), jnp.float32, 0.9, 0.999)
    s = u ** (1.0 / LRU_C)
    lru_lambda = jnp.log(s) - jnp.log1p(-s)
    sb_bias = jnp.linspace(SB_BIAS_HI, SB_BIAS_LO, HA, dtype=jnp.float32)[None, :] + _normal(ks[35], (DEPTH, HA), 0.1)
    return {
        "x_prompt": _normal(ks[1], (BATCH, SEQ, D_MODEL), 1.0),
        "x_sample": _normal(ks[2], (DEC_BATCH, DEC_SEQ, D_MODEL), 1.0),
        "cache_k": _normal(ks[3], (DEPTH, n_phys, PAGE_SIZE, HA, DA), 1.0),
        "cache_v": _normal(ks[4], (DEPTH, n_phys, PAGE_SIZE, HA, DA), 1.0),
        "state_lru_h": _normal(ks[5], (DEPTH, DEC_BATCH, WB), 0.5),
        "state_lru_conv": _normal(ks[6], (DEPTH, DEC_BATCH, CONV_B - 1, WB), 1.0),
        "state_conf_conv": _normal(ks[7], (DEPTH, DEC_BATCH, CONV_C - 1, WC), 1.0),
        "page_table": page_table,
        "w_in": _normal(ks[8], (DEPTH, D_MODEL, D_IN), D_MODEL ** -0.5),
        "b_merge": _normal(ks[9], (DEPTH, N_BRANCH, D_MODEL), 0.1),
        "sb_bias": sb_bias,
        "lru_conv_w": _normal(ks[10], (DEPTH, CONV_B, WB), CONV_B ** -0.5),
        "lru_conv_b": _normal(ks[11], (DEPTH, WB), 0.02),
        "lru_wa": _normal(ks[12], (DEPTH, NBLK_B, BB, BB), BB ** -0.5),
        "lru_ba": _normal(ks[14], (DEPTH, WB), 0.1),
        "lru_wx": _normal(ks[15], (DEPTH, NBLK_B, BB, BB), BB ** -0.5),
        "lru_bx": _normal(ks[16], (DEPTH, WB), 0.1),
        "lru_lambda": lru_lambda,
        "conf_conv_w": _normal(ks[17], (DEPTH, CONV_C, WC), CONV_C ** -0.5),
        "conf_conv_b": _normal(ks[18], (DEPTH, WC), 0.02),
        "conf_ln_g": 1.0 + _normal(ks[19], (DEPTH, WC), 0.05),
        "conf_ln_b": _normal(ks[20], (DEPTH, WC), 0.02),
        "w_branch": _normal(ks[21], (DEPTH, N_BRANCH, BRANCH_W, D_MODEL), BRANCH_W ** -0.5),
        "w_out": _normal(ks[22], (DEPTH, D_MODEL, D_MODEL), D_MODEL ** -0.5 * DN_BETA),
        "ln1_g": 1.0 + _normal(ks[23], (DEPTH, D_MODEL), 0.05),
        "ln1_b": _normal(ks[24], (DEPTH, D_MODEL), 0.02),
        "router_w": _normal(ks[25], (DEPTH, D_MODEL, N_EXPERTS), D_MODEL ** -0.5),
        "router_b": _normal(ks[26], (DEPTH, N_EXPERTS), 0.01),
        "moe_w_gate": _normal(ks[27], (DEPTH, N_EXPERTS, D_MODEL, D_FF), D_MODEL ** -0.5),
        "moe_b_gate": _normal(ks[28], (DEPTH, N_EXPERTS, D_FF), 0.02),
        "moe_w_up": _normal(ks[29], (DEPTH, N_EXPERTS, D_MODEL, D_FF), D_MODEL ** -0.5),
        "moe_b_up": _normal(ks[30], (DEPTH, N_EXPERTS, D_FF), 0.02),
        "moe_w_down": _normal(ks[31], (DEPTH, N_EXPERTS, D_FF, D_MODEL), D_FF ** -0.5 * DN_BETA),
        "moe_b_down": _normal(ks[32], (DEPTH, N_EXPERTS, D_MODEL), 0.02),
        "ln2_g": 1.0 + _normal(ks[33], (DEPTH, D_MODEL), 0.05),
        "ln2_b": _normal(ks[34], (DEPTH, D_MODEL), 0.02),
    }


def reference(x_prompt, x_sample, cache_k, cache_v, state_lru_h, state_lru_conv, state_conf_conv, page_table,
              w_in, b_merge, sb_bias, lru_conv_w, lru_conv_b, lru_wa, lru_ba, lru_wx, lru_bx, lru_lambda,
              conf_conv_w, conf_conv_b, conf_ln_g, conf_ln_b, w_branch, w_out, ln1_g, ln1_b,
              router_w, router_b, moe_w_gate, moe_b_gate, moe_w_up, moe_b_up, moe_w_down, moe_b_down,
              ln2_g, ln2_b):
    xp, xs = x_prompt, x_sample
    bp = xp.shape[0]
    kp, vp, ksm, vsm, hp, hsm, lcp, lcs, ccp, ccs = ([] for _ in range(10))
    for l in range(DEPTH):
        lw = (w_in[l], b_merge[l], sb_bias[l], lru_conv_w[l], lru_conv_b[l], lru_wa[l], lru_ba[l], lru_wx[l],
              lru_bx[l], lru_lambda[l], conf_conv_w[l], conf_conv_b[l], conf_ln_g[l], conf_ln_b[l], w_branch[l],
              w_out[l], ln1_g[l], ln1_b[l], router_w[l], router_b[l], moe_w_gate[l], moe_b_gate[l], moe_w_up[l],
              moe_b_up[l], moe_w_down[l], moe_b_down[l], ln2_g[l], ln2_b[l])
        xp, k, v, h, lb, cb = trunk_layer(
            xp, sb_prompt, jnp.zeros((bp, WB), xp.dtype), jnp.zeros((bp, CONV_B - 1, WB), xp.dtype),
            jnp.zeros((bp, CONV_C - 1, WC), xp.dtype), *lw)
        kp.append(k); vp.append(v); hp.append(h); lcp.append(lb); ccp.append(cb)
        attend = functools.partial(sb_sample, k_past=gather_pages(cache_k[l], page_table),
                                   v_past=gather_pages(cache_v[l], page_table))
        xs, k, v, h, lb, cb = trunk_layer(xs, attend, state_lru_h[l], state_lru_conv[l], state_conf_conv[l], *lw)
        ksm.append(k); vsm.append(v); hsm.append(h); lcs.append(lb); ccs.append(cb)
    return (xp, xs, jnp.stack(kp), jnp.stack(vp), jnp.stack(ksm), jnp.stack(vsm), jnp.stack(hp), jnp.stack(hsm),
            jnp.stack(lcp), jnp.stack(lcs), jnp.stack(ccp), jnp.stack(ccs))
```

```python
import functools

import jax
import jax.numpy as jnp
from jax import lax
from jax.experimental import pallas as pl
from jax.experimental.pallas import tpu as pltpu

F32 = jnp.float32
BF16 = jnp.bfloat16

LN_EPS = 1e-5
LRU_C = 8.0
SWIGLU_LIMIT = 7.0
SWIGLU_ALPHA = 1.702
TOP_K = 4
LANES = 128
SUBLANES = 8
VMEM_LIMIT = 56 * 1024 * 1024
EXPERT_ROWS = 256
NEG_INF = float("-inf")


def _sigmoid(x):
    return 1.0 / (1.0 + jnp.exp(-x))


def _neg_softplus(z):
    return -(jnp.maximum(z, 0.0) + jnp.log(1.0 + jnp.exp(-jnp.abs(z))))


def _layer_norm(y, g, b):
    mu = jnp.mean(y, axis=-1, keepdims=True)
    d = y - mu
    var = jnp.mean(d * d, axis=-1, keepdims=True)
    return d * lax.rsqrt(var + LN_EPS) * g + b


def _split_bf16(x):
    hi = x.astype(BF16)
    lo = (x - hi.astype(F32)).astype(BF16)
    return hi, lo


def _params(sem=None):
    return pltpu.CompilerParams(dimension_semantics=sem, vmem_limit_bytes=VMEM_LIMIT)


def _full_spec(a):
    nd = a.ndim
    return pl.BlockSpec(a.shape, lambda *_: (0,) * nd)


def _tokenwise_call(body, n_rows, tm, row_ins, const_ins, outs, name):
    in_specs = [pl.BlockSpec((tm, a.shape[1]), lambda i: (i, 0)) for a in row_ins]
    in_specs += [_full_spec(a) for a in const_ins]
    out_shape = [jax.ShapeDtypeStruct((n_rows, c), dt) for c, dt in outs]
    out_specs = [pl.BlockSpec((tm, c), lambda i: (i, 0)) for c, _ in outs]
    nr, nc = len(row_ins), len(const_ins)

    def kernel(*refs):
        body(refs[:nr], refs[nr:nr + nc], refs[nr + nc:])

    return pl.pallas_call(
        kernel, grid=(n_rows // tm,), in_specs=in_specs, out_specs=out_specs, out_shape=out_shape,
        compiler_params=_params(("parallel",)), name=name)(*row_ins, *const_ins)


def _proj_qkv_body(rows, consts, outs, *, wa, scale):
    (x_ref,), (w_ref,) = rows, consts
    q_ref, k_ref, v_ref, kb_ref, vb_ref = outs
    acc = jnp.dot(x_ref[...].astype(BF16), w_ref[...], preferred_element_type=F32)
    q_ref[...] = (acc[:, :wa] * scale).astype(BF16)
    k = acc[:, wa:2 * wa]
    v = acc[:, 2 * wa:3 * wa]
    k_ref[...] = k
    v_ref[...] = v
    kb_ref[...] = k.astype(BF16)
    vb_ref[...] = v.astype(BF16)


def _proj_mix_body(rows, consts, outs, *, wb, wc):
    (x_ref,), (w_ref,) = rows, consts
    lx_ref, glg_ref, glu_ref = outs
    acc = jnp.dot(x_ref[...].astype(BF16), w_ref[...], preferred_element_type=F32)
    lx_ref[...] = acc[:, :wb]
    glg_ref[...] = jax.nn.gelu(acc[:, wb:2 * wb])
    glu_ref[...] = acc[:, 2 * wb:2 * wb + wc] * _sigmoid(acc[:, 2 * wb + wc:])


def _sb_prompt_kernel(bias_ref, q_ref, k_ref, v_ref, tri_ref, o_init, o_ref, *, tq, dh):
    del o_init
    hp = pl.program_id(1)
    qi = pl.program_id(2)
    lane = lax.broadcasted_iota(jnp.int32, (1, 2 * dh), 1)
    q2 = q_ref[...]
    tri = tri_ref[...]
    row = lax.broadcasted_iota(jnp.int32, (tq, tq), 0)
    col = lax.broadcasted_iota(jnp.int32, (tq, tq), 1)
    causal = col < row
    accs = []
    for hh in range(2):
        qm = jnp.where((lane >= dh * hh) & (lane < dh * (hh + 1)), q2, jnp.zeros_like(q2))
        bias = bias_ref[2 * hp + hh]

        def tile(start, carry, masked, qm=qm, bias=bias):
            run, acc = carry
            kb = k_ref[pl.ds(start, tq), :]
            vb = v_ref[pl.ds(start, tq), :]
            z = lax.dot_general(qm, kb, (((1,), (1,)), ((), ())), preferred_element_type=F32) + bias
            lr = _neg_softplus(z)
            if masked:
                lr = jnp.where(causal, lr, 0.0)
            hi, lo = _split_bf16(lr)
            aft = (jnp.dot(hi, tri, preferred_element_type=F32)
                   + jnp.dot(lo, tri, preferred_element_type=F32))
            w = jnp.exp(z + lr + aft + run)
            if masked:
                w = jnp.where(causal, w, 0.0)
            acc = acc + jnp.dot(w.astype(BF16), vb, preferred_element_type=F32)
            run = run + aft[:, 0:1] + lr[:, 0:1]
            return run, acc

        carry = (jnp.zeros((tq, 1), F32), jnp.zeros((tq, 2 * dh), F32))
        carry = tile(pl.multiple_of(qi * tq, tq), carry, True)

        def body(jj, carry, tile=tile):
            return tile(pl.multiple_of((qi - 1 - jj) * tq, tq), carry, False)

        _, acc = lax.fori_loop(0, qi, body, carry)
        accs.append(acc)
    o_ref[...] = jnp.where(lane < dh, accs[0], accs[1]).astype(o_ref.dtype)


def _strict_lower_ones(n, dtype):
    s = lax.broadcasted_iota(jnp.int32, (n, n), 0)
    j = lax.broadcasted_iota(jnp.int32, (n, n), 1)
    return (s > j).astype(dtype)


def _sb_prompt(qs, kb, vb, sb_bias, o_init, *, batch, seq, heads, dh, tq):
    nt, wa = qs.shape
    nq = seq // tq
    tri = _strict_lower_ones(tq, BF16)
    grid_spec = pltpu.PrefetchScalarGridSpec(
        num_scalar_prefetch=1, grid=(batch, heads // 2, nq),
        in_specs=[
            pl.BlockSpec((tq, 2 * dh), lambda b, h, i, *_: (b * nq + i, h)),
            pl.BlockSpec((seq, 2 * dh), lambda b, h, i, *_: (b, h)),
            pl.BlockSpec((seq, 2 * dh), lambda b, h, i, *_: (b, h)),
            pl.BlockSpec((tq, tq), lambda b, h, i, *_: (0, 0)),
            pl.BlockSpec(memory_space=pl.ANY),
        ],
        out_specs=pl.BlockSpec((tq, 2 * dh), lambda b, h, i, *_: (b * nq + i, h)),
    )
    return pl.pallas_call(
        functools.partial(_sb_prompt_kernel, tq=tq, dh=dh), grid_spec=grid_spec,
        out_shape=jax.ShapeDtypeStruct((nt, wa), BF16), input_output_aliases={5: 0},
        compiler_params=_params(("parallel", "parallel", "arbitrary")), name="sb_prompt")(
            sb_bias, qs, kb, vb, tri, o_init)


def _sb_sample_kernel(pt_ref, bias_ref, qm_ref, kn_ref, vn_ref, hm_ref, tri_ref, *refs,
                      pages_per_step, page, past):
    k_refs = refs[:pages_per_step]
    v_refs = refs[pages_per_step:2 * pages_per_step]
    o_ref, run_ref, acc_ref = refs[2 * pages_per_step:]
    del pt_ref
    b = pl.program_id(0)
    g = pl.program_id(1)
    qm = qm_ref[0]
    bias = bias_ref[...]
    qmb = qm.astype(BF16)
    tri = tri_ref[...]

    @pl.when(g == 0)
    def _():
        z = jnp.sum(qm * kn_ref[0], axis=-1, keepdims=True) + bias
        visible = jnp.full(z.shape, past, jnp.int32) < jnp.full(z.shape, past, jnp.int32)
        lr = jnp.where(visible, _neg_softplus(z), 0.0)
        w = jnp.where(visible, jnp.exp(z + lr), 0.0)
        run_ref[...] = lr
        acc_ref[...] = w * vn_ref[0]

    run = run_ref[...]
    acc = acc_ref[...]
    for s in range(pages_per_step):
        kb = k_refs[s][0, 0].astype(BF16)
        vb = v_refs[s][0, 0].astype(BF16)
        z = lax.dot_general(qmb, kb, (((1,), (1,)), ((), ())), preferred_element_type=F32) + bias
        lr = _neg_softplus(z)
        hi, lo = _split_bf16(lr)
        aft = (jnp.dot(hi, tri, preferred_element_type=F32)
               + jnp.dot(lo, tri, preferred_element_type=F32))
        w = jnp.exp(z + lr + aft + run)
        acc = acc + jnp.dot(w.astype(BF16), vb, preferred_element_type=F32)
        run = run + aft[:, 0:1] + lr[:, 0:1]
    run_ref[...] = run
    acc_ref[...] = acc

    @pl.when(g == pl.num_programs(1) - 1)
    def _():
        o_ref[pl.ds(b, 1), :] = jnp.sum(acc * hm_ref[...], axis=0, keepdims=True)


def _sb_sample(q, k_new, v_new, cache_k, cache_v, page_table, sb_bias, *, layer, heads, dh, pages_per_step):
    s_rows, wa = q.shape
    n_pages = page_table.shape[1]
    page = cache_k.shape[2]
    n_phys = cache_k.shape[1]
    ck = cache_k.reshape(cache_k.shape[0], n_phys, page, wa)
    cv = cache_v.reshape(cache_v.shape[0], n_phys, page, wa)
    hrows = 2 * heads
    hsel = (lax.broadcasted_iota(jnp.int32, (hrows, wa), 1) // dh
            == lax.broadcasted_iota(jnp.int32, (hrows, wa), 0)).astype(F32)
    qm = q[:, None, :] * hsel[None]
    bias = jnp.concatenate([sb_bias, jnp.zeros((hrows - heads,), F32)])[:, None]
    tri = _strict_lower_ones(page, BF16)
    steps = n_pages // pages_per_step

    def page_map(s):
        def index_map(b, g, pt):
            return (layer, pt[b, n_pages - 1 - (g * pages_per_step + s)], 0, 0)
        return index_map

    page_specs = [pl.BlockSpec((1, 1, page, wa), page_map(s)) for s in range(pages_per_step)]
    grid_spec = pltpu.PrefetchScalarGridSpec(
        num_scalar_prefetch=1, grid=(s_rows, steps),
        in_specs=[
            pl.BlockSpec((hrows, 1), lambda b, g, pt: (0, 0)),
            pl.BlockSpec((1, hrows, wa), lambda b, g, pt: (b, 0, 0)),
            pl.BlockSpec((1, 1, wa), lambda b, g, pt: (b, 0, 0)),
            pl.BlockSpec((1, 1, wa), lambda b, g, pt: (b, 0, 0)),
            pl.BlockSpec((hrows, wa), lambda b, g, pt: (0, 0)),
            pl.BlockSpec((page, page), lambda b, g, pt: (0, 0)),
        ] + page_specs + page_specs,
        out_specs=pl.BlockSpec((s_rows, wa), lambda b, g, pt: (0, 0)),
        scratch_shapes=[pltpu.VMEM((hrows, 1), F32), pltpu.VMEM((hrows, wa), F32)],
    )
    return pl.pallas_call(
        functools.partial(_sb_sample_kernel, pages_per_step=pages_per_step, page=page,
                          past=n_pages * page),
        grid_spec=grid_spec, out_shape=jax.ShapeDtypeStruct((s_rows, wa), F32),
        compiler_params=_params(("arbitrary", "arbitrary")), name="sb_sample")(
            page_table, bias, qm, k_new[:, None, :], v_new[:, None, :], hsel, tri,
            *([ck] * pages_per_step), *([cv] * pages_per_step))


def _lru_gates(xc, wa_ref, ba_ref, wx_ref, bx_ref, lam_ref):
    xb = xc.astype(BF16)
    r = _sigmoid(jnp.dot(xb, wa_ref[...], preferred_element_type=F32) + ba_ref[...])
    i = _sigmoid(jnp.dot(xb, wx_ref[...], preferred_element_type=F32) + bx_ref[...])
    log_a = LRU_C * r * _neg_softplus(-lam_ref[...])
    a = jnp.exp(log_a)
    u = jnp.sqrt(-jnp.tanh(log_a) * (a * a + 1.0)) * (i * xc)
    return a, u


def _lru_prompt_kernel(lx_ref, glg_ref, cw_ref, cb_ref, wa_ref, ba_ref, wx_ref, bx_ref, lam_ref, ob_init,
                       ob_ref, h_ref, buf_ref, xpad, a_s, u_s, *, seq, chunk, taps):
    del ob_init
    width = lx_ref.shape[1]
    xpad[0:SUBLANES, :] = jnp.zeros((SUBLANES, width), F32)
    xpad[SUBLANES:, :] = lx_ref[...]
    rowid = lax.broadcasted_iota(jnp.int32, (SUBLANES, width), 0)
    h = jnp.zeros((1, width), F32)
    for c in range(seq // chunk):
        base = SUBLANES + c * chunk
        xc = cb_ref[...] + cw_ref[taps - 1:taps, :] * xpad[base:base + chunk, :]
        for d in range(1, taps):
            xc = xc + cw_ref[taps - 1 - d:taps - d, :] * xpad[base - d:base - d + chunk, :]
        a, u = _lru_gates(xc, wa_ref, ba_ref, wx_ref, bx_ref, lam_ref)
        a_s[...] = a
        u_s[...] = u

        def group(gi, h):
            r0 = pl.multiple_of(gi * SUBLANES, SUBLANES)
            ga = a_s[pl.ds(r0, SUBLANES), :]
            gu = u_s[pl.ds(r0, SUBLANES), :]
            for d in (1, 2, 4):
                keep = rowid >= d
                gu = jnp.where(keep, ga * pltpu.roll(gu, d, 0) + gu, gu)
                ga = jnp.where(keep, ga * pltpu.roll(ga, d, 0), ga)
            hs = ga * h + gu
            u_s[pl.ds(r0, SUBLANES), :] = hs
            return hs[SUBLANES - 1:SUBLANES, :]

        h = lax.fori_loop(0, chunk // SUBLANES, group, h)
        rows = slice(c * chunk, (c + 1) * chunk)
        ob_ref[rows, :] = (u_s[...] * glg_ref[rows, :].astype(F32)).astype(ob_ref.dtype)
    h_ref[0] = h
    buf_ref[0] = lx_ref[seq - (taps - 1):seq, :]


def _lru_prompt(lx, glg, cw, cb, wa_bd, ba, wx_bd, bx, lam, ob_init, *, batch, seq, chunk):
    nt, width = lx.shape
    taps = cw.shape[0]
    consts = [cw, cb, wa_bd, ba, wx_bd, bx, lam]
    return pl.pallas_call(
        functools.partial(_lru_prompt_kernel, seq=seq, chunk=chunk, taps=taps),
        grid=(batch,),
        in_specs=[pl.BlockSpec((seq, width), lambda b: (b, 0)),
                  pl.BlockSpec((seq, width), lambda b: (b, 0))] + [_full_spec(a) for a in consts]
        + [pl.BlockSpec(memory_space=pl.ANY)],
        input_output_aliases={2 + len(consts): 0},
        out_specs=[pl.BlockSpec((seq, width), lambda b: (b, 0)),
                   pl.BlockSpec((1, 1, width), lambda b: (b, 0, 0)),
                   pl.BlockSpec((1, taps - 1, width), lambda b: (b, 0, 0))],
        out_shape=[jax.ShapeDtypeStruct((nt, width), BF16),
                   jax.ShapeDtypeStruct((batch, 1, width), F32),
                   jax.ShapeDtypeStruct((batch, taps - 1, width), F32)],
        scratch_shapes=[pltpu.VMEM((seq + SUBLANES, width), F32),
                        pltpu.VMEM((chunk, width), F32), pltpu.VMEM((chunk, width), F32)],
        compiler_params=_params(("parallel",)), name="lru_prompt")(lx, glg, *consts, ob_init)


def _conf_prompt_kernel(glu_ref, cw_ref, cb_ref, g_ref, b_ref, oc_init, oc_ref, buf_ref, xpad, win, cc_s,
                        *, seq, chunk, taps, pad):
    del oc_init
    width = glu_ref.shape[1]
    xpad[0:pad, :] = jnp.zeros((pad, width), F32)
    xpad[pad:, :] = glu_ref[...]
    lead = pad - (taps - 1)

    def one_chunk(c, carry):
        t0 = pl.multiple_of(c * chunk, chunk)
        win[...] = xpad[pl.ds(t0, chunk + pad), :]
        for l0 in range(0, width, LANES):
            lanes = slice(l0, l0 + LANES)
            acc = cb_ref[:, lanes] + cw_ref[0:1, lanes] * win[lead:lead + chunk, lanes]
            for j in range(1, taps):
                acc = acc + cw_ref[j:j + 1, lanes] * win[lead + j:lead + j + chunk, lanes]
            cc_s[:, lanes] = acc
        y = _layer_norm(cc_s[...], g_ref[...], b_ref[...])
        oc_ref[pl.ds(t0, chunk), :] = (y * _sigmoid(y)).astype(oc_ref.dtype)
        return carry

    lax.fori_loop(0, seq // chunk, one_chunk, 0)
    buf_ref[0] = glu_ref[seq - (taps - 1):seq, :]


def _conf_prompt(glu, cw, cb, g, b, oc_init, *, batch, seq, chunk):
    nt, width = glu.shape
    taps = cw.shape[0]
    pad = -(-(taps - 1) // SUBLANES) * SUBLANES
    consts = [cw, cb, g, b]
    return pl.pallas_call(
        functools.partial(_conf_prompt_kernel, seq=seq, chunk=chunk, taps=taps, pad=pad),
        grid=(batch,),
        in_specs=[pl.BlockSpec((seq, width), lambda b: (b, 0))] + [_full_spec(a) for a in consts]
        + [pl.BlockSpec(memory_space=pl.ANY)],
        input_output_aliases={1 + len(consts): 0},
        out_specs=[pl.BlockSpec((seq, width), lambda b: (b, 0)),
                   pl.BlockSpec((1, taps - 1, width), lambda b: (b, 0, 0))],
        out_shape=[jax.ShapeDtypeStruct((nt, width), BF16),
                   jax.ShapeDtypeStruct((batch, taps - 1, width), F32)],
        scratch_shapes=[pltpu.VMEM((seq + pad, width), F32), pltpu.VMEM((chunk + pad, width), F32),
                        pltpu.VMEM((chunk, width), F32)],
        compiler_params=_params(("parallel",)), name="conf_prompt")(glu, *consts, oc_init)


def _sample_mix_kernel(lx_ref, glg_ref, glu_ref, oas_ref, h0_ref, lbuf_ref, cbuf_ref,
                       lcw_ref, lcb_ref, wa_ref, ba_ref, wx_ref, bx_ref, lam_ref,
                       ccw_ref, ccb_ref, cg_ref, cbeta_ref, oa_in, ob_in, oc_in,
                       oa_ref, ob_ref, oc_ref, h_ref, lbuf_out, cbuf_out, *, ltaps, ctaps):
    del oa_in, ob_in, oc_in
    oa_ref[...] = oas_ref[...].astype(oa_ref.dtype)
    lx = lx_ref[...]
    xc = lcb_ref[...] + lcw_ref[ltaps - 1:ltaps, :] * lx
    for j in range(ltaps - 1):
        xc = xc + lcw_ref[j:j + 1, :] * lbuf_ref[j]
    a, u = _lru_gates(xc, wa_ref, ba_ref, wx_ref, bx_ref, lam_ref)
    h = a * h0_ref[...] + u
    h_ref[...] = h
    ob_ref[...] = (h * glg_ref[...].astype(F32)).astype(ob_ref.dtype)
    for j in range(ltaps - 2):
        lbuf_out[j] = lbuf_ref[j + 1]
    lbuf_out[ltaps - 2] = lx

    glu = glu_ref[...]
    cc = ccb_ref[...] + ccw_ref[ctaps - 1:ctaps, :] * glu
    for j in range(ctaps - 1):
        cc = cc + ccw_ref[j:j + 1, :] * cbuf_ref[j]
    y = _layer_norm(cc, cg_ref[...], cbeta_ref[...])
    oc_ref[...] = (y * _sigmoid(y)).astype(oc_ref.dtype)
    for j in range(ctaps - 2):
        cbuf_out[j] = cbuf_ref[j + 1]
    cbuf_out[ctaps - 2] = glu


def _sample_mix(lx, glg, glu, oa_s, h0, lbuf, cbuf, lru_consts, conf_consts, oa_all, ob_all, oc_all, *, row0, rows):
    width = lx.shape[1]
    blk = row0 // rows
    ltaps = lru_consts[0].shape[0]
    ctaps = conf_consts[0].shape[0]
    lbuf_t = jnp.transpose(lbuf, (1, 0, 2))
    cbuf_t = jnp.transpose(cbuf, (1, 0, 2))
    consts = list(lru_consts) + list(conf_consts)
    tail = pl.BlockSpec((rows, width), lambda i: (blk, 0))
    any_spec = pl.BlockSpec(memory_space=pl.ANY)
    ins = [lx, glg, glu, oa_s, h0, lbuf_t, cbuf_t] + consts + [oa_all, ob_all, oc_all]
    in_specs = ([tail, tail, tail, _full_spec(oa_s), _full_spec(h0), _full_spec(lbuf_t), _full_spec(cbuf_t)]
                + [_full_spec(a) for a in consts] + [any_spec, any_spec, any_spec])
    n_in = len(ins)
    oa, ob, oc, h, lb, cb = pl.pallas_call(
        functools.partial(_sample_mix_kernel, ltaps=ltaps, ctaps=ctaps),
        grid=(1,), in_specs=in_specs,
        out_specs=[tail, tail, tail, _full_spec(h0), _full_spec(lbuf_t), _full_spec(cbuf_t)],
        out_shape=[jax.ShapeDtypeStruct(a.shape, a.dtype) for a in (oa_all, ob_all, oc_all)]
        + [jax.ShapeDtypeStruct(h0.shape, F32), jax.ShapeDtypeStruct(lbuf_t.shape, F32),
           jax.ShapeDtypeStruct(cbuf_t.shape, F32)],
        input_output_aliases={n_in - 3: 0, n_in - 2: 1, n_in - 1: 2},
        compiler_params=_params(("arbitrary",)), name="sample_mix")(*ins)
    return oa, ob, oc, h, jnp.transpose(lb, (1, 0, 2)), jnp.transpose(cb, (1, 0, 2))


def _merge_body(rows, consts, outs, *, alpha, n_experts, d_model):
    x_ref, oa_ref, ob_ref, oc_ref = rows
    wgate_ref, bgate_ref, wbr_ref, wout_ref, g_ref, b_ref, rw_ref, rb_ref = consts
    x1_ref, ridx_ref, rgw_ref = outs
    x = x_ref[...]
    xb = x.astype(BF16)
    m = None
    for i, o_ref in enumerate((oa_ref, ob_ref, oc_ref)):
        cols = slice(i * d_model, (i + 1) * d_model)
        gate = _sigmoid(jnp.dot(xb, wgate_ref[:, cols], preferred_element_type=F32) + bgate_ref[:, cols])
        p = gate * jnp.dot(o_ref[...], wbr_ref[i], preferred_element_type=F32)
        m = p if m is None else m + p
    y = alpha * x + jnp.dot(m.astype(BF16), wout_ref[...], preferred_element_type=F32)
    x1 = _layer_norm(y, g_ref[...], b_ref[...])
    x1_ref[...] = x1
    logits = jnp.dot(x1.astype(BF16), rw_ref[...], preferred_element_type=F32) + rb_ref[...]
    lane = lax.broadcasted_iota(jnp.int32, logits.shape, 1)
    lanef = lane.astype(F32)
    logits = jnp.where(lane < n_experts, logits, NEG_INF)
    idx_out = jnp.zeros(logits.shape, F32)
    val_out = jnp.zeros(logits.shape, F32)
    top = None
    for k in range(TOP_K):
        mx = jnp.max(logits, axis=-1, keepdims=True)
        sel = jnp.min(jnp.where(logits == mx, lanef, float(LANES)), axis=-1, keepdims=True)
        top = mx if top is None else top
        idx_out = jnp.where(lane == k, sel, idx_out)
        val_out = jnp.where(lane == k, jnp.exp(mx - top), val_out)
        logits = jnp.where(lanef == sel, NEG_INF, logits)
    ridx_ref[...] = idx_out.astype(jnp.int32)
    rgw_ref[...] = val_out / jnp.sum(val_out, axis=-1, keepdims=True)


def _dispatch_kernel(dest_hbm, x_ref, xs_in, xs_hbm, dest_s, sem_i, sem_r, *, tt):
    del xs_in
    i = pl.program_id(0)
    cp = pltpu.make_async_copy(dest_hbm.at[i], dest_s, sem_i)
    cp.start()
    cp.wait()

    def row_copy(a):
        return pltpu.make_async_copy(x_ref.at[pl.ds(a // TOP_K, 1), :], xs_hbm.at[pl.ds(dest_s[0, a], 1), :], sem_r)

    def start(a, c):
        row_copy(a).start()
        return c

    def wait(a, c):
        row_copy(a).wait()
        return c

    lax.fori_loop(0, tt * TOP_K, start, 0)
    lax.fori_loop(0, tt * TOP_K, wait, 0)


def _dispatch(x1, dest2, xs_zero, *, tt):
    nt, d = x1.shape
    return pl.pallas_call(
        functools.partial(_dispatch_kernel, tt=tt), grid=(nt // tt,),
        in_specs=[pl.BlockSpec(memory_space=pl.ANY), pl.BlockSpec((tt, d), lambda i: (i, 0)),
                  pl.BlockSpec(memory_space=pl.ANY)],
        out_specs=pl.BlockSpec(memory_space=pl.ANY),
        out_shape=jax.ShapeDtypeStruct(xs_zero.shape, xs_zero.dtype),
        scratch_shapes=[pltpu.SMEM((1, tt * TOP_K), jnp.int32), pltpu.SemaphoreType.DMA, pltpu.SemaphoreType.DMA],
        input_output_aliases={2: 0},
        compiler_params=_params(("arbitrary",)), name="moe_dispatch")(dest2, x1, xs_zero)


def _expert_kernel(be_ref, nu_ref, x_ref, wg_ref, bg_ref, wu_ref, bu_ref, wd_ref, bd_ref, y_ref):
    i = pl.program_id(0)

    @pl.when(i < nu_ref[0])
    def _():
        xb = x_ref[...].astype(BF16)
        g = jnp.minimum(jnp.dot(xb, wg_ref[0], preferred_element_type=F32) + bg_ref[0], SWIGLU_LIMIT)
        u = jnp.clip(jnp.dot(xb, wu_ref[0], preferred_element_type=F32) + bu_ref[0], -SWIGLU_LIMIT, SWIGLU_LIMIT)
        h = (u + 1.0) * (g * _sigmoid(SWIGLU_ALPHA * g))
        y_ref[...] = jnp.dot(h.astype(BF16), wd_ref[0], preferred_element_type=F32) + bd_ref[0]

    @pl.when(i >= nu_ref[0])
    def _():
        y_ref[...] = jnp.zeros(y_ref.shape, y_ref.dtype)


def _experts(xs, block_e, n_used, wg, bg, wu, bu, wd, bd):
    rows, d = xs.shape
    dff = wg.shape[2]
    nb = rows // EXPERT_ROWS
    wspec = lambda shape: pl.BlockSpec((1,) + shape, lambda i, be, nu: (be[i], 0, 0))
    grid_spec = pltpu.PrefetchScalarGridSpec(
        num_scalar_prefetch=2, grid=(nb,),
        in_specs=[pl.BlockSpec((EXPERT_ROWS, d), lambda i, be, nu: (i, 0)),
                  wspec((d, dff)), wspec((1, dff)), wspec((d, dff)), wspec((1, dff)),
                  wspec((dff, d)), wspec((1, d))],
        out_specs=pl.BlockSpec((EXPERT_ROWS, d), lambda i, be, nu: (i, 0)))
    return pl.pallas_call(
        _expert_kernel, grid_spec=grid_spec, out_shape=jax.ShapeDtypeStruct((rows, d), F32),
        compiler_params=_params(("arbitrary",)), name="moe_experts")(
            block_e, n_used, xs, wg, bg[:, None, :], wu, bu[:, None, :], wd, bd[:, None, :])


def _combine_kernel(dest_hbm, ys_hbm, x1_ref, gw_ref, g_ref, b_ref, x2_ref, dest_s, ybuf, sem_i, sem_r, *, tt, alpha):
    i = pl.program_id(0)
    cp = pltpu.make_async_copy(dest_hbm.at[i], dest_s, sem_i)
    cp.start()
    cp.wait()

    def row_copy(a):
        return pltpu.make_async_copy(ys_hbm.at[pl.ds(dest_s[0, a], 1), :],
                                     ybuf.at[a % TOP_K, pl.ds(a // TOP_K, 1), :], sem_r)

    def start(a, c):
        row_copy(a).start()
        return c

    def wait(a, c):
        row_copy(a).wait()
        return c

    lax.fori_loop(0, tt * TOP_K, start, 0)
    lax.fori_loop(0, tt * TOP_K, wait, 0)
    gw = gw_ref[...]
    moe = gw[:, 0:1] * ybuf[0]
    for k in range(1, TOP_K):
        moe = moe + gw[:, k:k + 1] * ybuf[k]
    x2_ref[...] = _layer_norm(alpha * x1_ref[...] + moe, g_ref[...], b_ref[...])


def _combine(ys, dest2, x1, rgw, g, b, *, tt, alpha):
    nt, d = x1.shape
    return pl.pallas_call(
        functools.partial(_combine_kernel, tt=tt, alpha=alpha), grid=(nt // tt,),
        in_specs=[pl.BlockSpec(memory_space=pl.ANY), pl.BlockSpec(memory_space=pl.ANY),
                  pl.BlockSpec((tt, d), lambda i: (i, 0)), pl.BlockSpec((tt, LANES), lambda i: (i, 0)),
                  _full_spec(g), _full_spec(b)],
        out_specs=pl.BlockSpec((tt, d), lambda i: (i, 0)),
        out_shape=jax.ShapeDtypeStruct((nt, d), F32),
        scratch_shapes=[pltpu.SMEM((1, tt * TOP_K), jnp.int32), pltpu.VMEM((TOP_K, tt, d), F32),
                        pltpu.SemaphoreType.DMA, pltpu.SemaphoreType.DMA],
        compiler_params=_params(("arbitrary",)), name="moe_combine")(dest2, ys, x1, rgw, g, b)


def _route(ridx, n_experts, tt):
    nt = ridx.shape[0]
    n_assign = nt * TOP_K
    flat_e = ridx[:, :TOP_K].reshape(n_assign)
    onehot = (flat_e[:, None] == jnp.arange(n_experts, dtype=jnp.int32)[None, :]).astype(jnp.int32)
    csum = jnp.cumsum(onehot, axis=0)
    counts = csum[-1]
    padded = (counts + EXPERT_ROWS - 1) // EXPERT_ROWS * EXPERT_ROWS
    pend = jnp.cumsum(padded)
    pstart = pend - padded
    dest = jnp.sum(onehot * (csum - 1 + pstart[None, :]), axis=1).astype(jnp.int32)
    n_blocks = -(-n_assign // EXPERT_ROWS) + n_experts
    block_e = jnp.clip(jnp.searchsorted(pend, jnp.arange(n_blocks, dtype=jnp.int32) * EXPERT_ROWS, side="right"),
                       0, n_experts - 1).astype(jnp.int32)
    n_used = (pend[-1:] // EXPERT_ROWS).astype(jnp.int32)
    return dest.reshape(nt // tt, 1, tt * TOP_K), block_e, n_used, n_blocks


def _block_diag(w):
    nb, bi, bo = w.shape
    eye = jnp.eye(nb, dtype=w.dtype)
    return (w[:, :, None, :] * eye[:, None, :, None]).reshape(nb * bi, nb * bo)


def _largest_tile(n, limit, align):
    best = align
    for t in range(align, limit + 1, align):
        if n % t == 0:
            best = t
    return best


def kernel(x_prompt, x_sample, cache_k, cache_v, state_lru_h, state_lru_conv, state_conf_conv, page_table, w_in, b_merge, sb_bias, lru_conv_w, lru_conv_b, lru_wa, lru_ba, lru_wx, lru_bx, lru_lambda, conf_conv_w, conf_conv_b, conf_ln_g, conf_ln_b, w_branch, w_out, ln1_g, ln1_b, router_w, router_b, moe_w_gate, moe_b_gate, moe_w_up, moe_b_up, moe_w_down, moe_b_down, ln2_g, ln2_b):
    batch, seq, d_model = x_prompt.shape
    s_rows = x_sample.shape[0]
    depth = w_in.shape[0]
    heads, dh = cache_k.shape[3], cache_k.shape[4]
    wa = heads * dh
    wb = lru_conv_w.shape[2]
    wc = conf_conv_w.shape[2]
    n_experts = router_w.shape[2]
    n_prompt = batch * seq
    nt = n_prompt + s_rows
    alpha = float((2 * depth) ** 0.25)
    tm = _largest_tile(nt, 640, 16)
    tt = _largest_tile(nt, 320, 8)
    row = lambda v: v[None, :]

    x = jnp.concatenate([x_prompt.reshape(n_prompt, d_model), x_sample.reshape(s_rows, d_model)], axis=0)
    outs = {k: [] for k in ("kp", "vp", "ks", "vs", "hp", "hs", "lp", "ls", "cp", "cs")}
    for l in range(depth):
        w_l = w_in[l].astype(BF16)
        c0, c1, c2 = 3 * wa, 3 * wa + 2 * wb + 2 * wc, w_in.shape[2]
        qs, k, v, kb, vb = _tokenwise_call(
            functools.partial(_proj_qkv_body, wa=wa, scale=dh ** -0.5), nt, tm, [x], [w_l[:, :c0]],
            [(wa, BF16), (wa, F32), (wa, F32), (wa, BF16), (wa, BF16)], "proj_qkv")
        lx, glg, glu = _tokenwise_call(
            functools.partial(_proj_mix_body, wb=wb, wc=wc), nt, tm, [x], [w_l[:, c0:c1]],
            [(wb, F32), (wb, F32), (wc, F32)], "proj_mix")

        o_a = _sb_prompt(qs, kb, vb, sb_bias[l], jnp.zeros((nt, wa), BF16), batch=batch, seq=seq, heads=heads,
                         dh=dh, tq=256)
        q_s = qs[n_prompt:].astype(F32)
        oa_s = _sb_sample(q_s, k[n_prompt:], v[n_prompt:], cache_k, cache_v, page_table, sb_bias[l],
                          layer=l, heads=heads, dh=dh, pages_per_step=8)

        lru_consts = [lru_conv_w[l], row(lru_conv_b[l]), _block_diag(lru_wa[l]).astype(BF16), row(lru_ba[l]),
                      _block_diag(lru_wx[l]).astype(BF16), row(lru_bx[l]), row(lru_lambda[l])]
        conf_consts = [conf_conv_w[l], row(conf_conv_b[l]), row(conf_ln_g[l]), row(conf_ln_b[l])]
        o_b, h_p, lbuf_p = _lru_prompt(lx, glg, *lru_consts, jnp.zeros((nt, wb), BF16), batch=batch, seq=seq,
                                       chunk=256)
        o_c, cbuf_p = _conf_prompt(glu, *conf_consts, jnp.zeros((nt, wc), BF16), batch=batch, seq=seq, chunk=128)
        o_a, o_b, o_c, h_s, lbuf_s, cbuf_s = _sample_mix(
            lx, glg, glu, oa_s, state_lru_h[l], state_lru_conv[l], state_conf_conv[l], lru_consts, conf_consts,
            o_a, o_b, o_c, row0=n_prompt, rows=s_rows)

        rw = jnp.zeros((d_model, LANES), BF16).at[:, :n_experts].set(router_w[l].astype(BF16))
        rb = jnp.zeros((1, LANES), F32).at[0, :n_experts].set(router_b[l])
        x1, ridx, rgw = _tokenwise_call(
            functools.partial(_merge_body, alpha=alpha, n_experts=n_experts, d_model=d_model), nt, tm,
            [x, o_a, o_b, o_c],
            [w_l[:, c1:c2], b_merge[l].reshape(1, c2 - c1), w_branch[l].astype(BF16), w_out[l].astype(BF16),
             row(ln1_g[l]), row(ln1_b[l]), rw, rb],
            [(d_model, F32), (LANES, jnp.int32), (LANES, F32)], "merge")

        dest2, block_e, n_used, n_blocks = _route(ridx, n_experts, tt)
        xs = _dispatch(x1, dest2, jnp.zeros((n_blocks * EXPERT_ROWS, d_model), F32), tt=tt)
        ys = _experts(xs, block_e, n_used, moe_w_gate[l].astype(BF16), moe_b_gate[l], moe_w_up[l].astype(BF16),
                      moe_b_up[l], moe_w_down[l].astype(BF16), moe_b_down[l])
        x = _combine(ys, dest2, x1, rgw, row(ln2_g[l]), row(ln2_b[l]), tt=tt, alpha=alpha)

        outs["kp"].append(k[:n_prompt].reshape(batch, seq, heads, dh))
        outs["vp"].append(v[:n_prompt].reshape(batch, seq, heads, dh))
        outs["ks"].append(k[n_prompt:].reshape(s_rows, 1, heads, dh))
        outs["vs"].append(v[n_prompt:].reshape(s_rows, 1, heads, dh))
        outs["hp"].append(h_p.reshape(batch, wb))
        outs["hs"].append(h_s)
        outs["lp"].append(lbuf_p)
        outs["ls"].append(lbuf_s)
        outs["cp"].append(cbuf_p)
        outs["cs"].append(cbuf_s)

    st = {k_: jnp.stack(v_) for k_, v_ in outs.items()}
    return (x[:n_prompt].reshape(batch, seq, d_model), x[n_prompt:].reshape(s_rows, 1, d_model),
            st["kp"], st["vp"], st["ks"], st["vs"], st["hp"], st["hs"], st["lp"], st["ls"], st["cp"], st["cs"])
```

```python
import functools

import jax
import jax.numpy as jnp
from jax import lax
from jax.experimental import pallas as pl
from jax.experimental.pallas import tpu as pltpu

F32 = jnp.float32
BF16 = jnp.bfloat16

LN_EPS = 1e-5
LRU_C = 8.0
SWIGLU_LIMIT = 7.0
SWIGLU_ALPHA = 1.702
TOP_K = 4
LANES = 128
SUBLANES = 8
VMEM_LIMIT = 56 * 1024 * 1024
EXPERT_ROWS = 256
SB_HEADS_PER_STEP = 4
ROW_UNROLL = 8
NEG_INF = float("-inf")
LOG2E = 1.4426950408889634


def _sigmoid(x):
    return 1.0 / (1.0 + jnp.exp(-x))


def _neg_softplus(z):
    return -(jnp.maximum(z, 0.0) + jnp.log(1.0 + jnp.exp(-jnp.abs(z))))


def _neg_abs(x):
    bits = lax.bitcast_convert_type(x, jnp.uint32) | jnp.uint32(0x80000000)
    return lax.bitcast_convert_type(bits, F32)


def _layer_norm(y, g, b):
    mu = jnp.mean(y, axis=-1, keepdims=True)
    d = y - mu
    var = jnp.mean(d * d, axis=-1, keepdims=True)
    return d * lax.rsqrt(var + LN_EPS) * g + b


def _split_bf16(x):
    hi = x.astype(BF16)
    lo = (x - hi.astype(F32)).astype(BF16)
    return hi, lo


def _params(sem=None):
    return pltpu.CompilerParams(dimension_semantics=sem, vmem_limit_bytes=VMEM_LIMIT)


def _full_spec(a):
    nd = a.ndim
    return pl.BlockSpec(a.shape, lambda *_: (0,) * nd)


def _tokenwise_call(body, n_rows, tm, row_ins, const_ins, outs, name):
    in_specs = [pl.BlockSpec((tm, a.shape[1]), lambda i: (i, 0)) for a in row_ins]
    in_specs += [_full_spec(a) for a in const_ins]
    out_shape = [jax.ShapeDtypeStruct((n_rows, c), dt) for c, dt in outs]
    out_specs = [pl.BlockSpec((tm, c), lambda i: (i, 0)) for c, _ in outs]
    nr, nc = len(row_ins), len(const_ins)

    def kernel(*refs):
        body(refs[:nr], refs[nr:nr + nc], refs[nr + nc:])

    return pl.pallas_call(
        kernel, grid=(n_rows // tm,), in_specs=in_specs, out_specs=out_specs, out_shape=out_shape,
        compiler_params=_params(("parallel",)), name=name)(*row_ins, *const_ins)


def _proj_qkv_body(rows, consts, outs, *, wa, scale):
    (x_ref,), (w_ref,) = rows, consts
    q_ref, k_ref, v_ref, kb_ref, vb_ref = outs
    acc = jnp.dot(x_ref[...].astype(BF16), w_ref[...], preferred_element_type=F32)
    q_ref[...] = (acc[:, :wa] * scale).astype(BF16)
    k = acc[:, wa:2 * wa]
    v = acc[:, 2 * wa:3 * wa]
    k_ref[...] = k
    v_ref[...] = v
    kb_ref[...] = k.astype(BF16)
    vb_ref[...] = v.astype(BF16)


def _proj_mix_body(rows, consts, outs, *, wb, wc):
    (x_ref,), (w_ref,) = rows, consts
    lx_ref, glg_ref, glu_ref = outs
    acc = jnp.dot(x_ref[...].astype(BF16), w_ref[...], preferred_element_type=F32)
    lx_ref[...] = acc[:, :wb]
    glg_ref[...] = jax.nn.gelu(acc[:, wb:2 * wb])
    glu_ref[...] = acc[:, 2 * wb:2 * wb + wc] * _sigmoid(acc[:, 2 * wb + wc:])


def _sb_prompt_kernel(bias_ref, q_ref, k_ref, v_ref, tri_ref, o_init, o_ref, *, tq, dh, heads_per_step):
    del o_init
    hg = pl.program_id(1)
    qi = pl.program_id(2)
    pair_w = 2 * dh
    lane = lax.broadcasted_iota(jnp.int32, (1, pair_w), 1)
    tri = tri_ref[...]
    row = lax.broadcasted_iota(jnp.int32, (tq, tq), 0)
    col = lax.broadcasted_iota(jnp.int32, (tq, tq), 1)
    causal = col < row
    qms, biases, cols = [], [], []
    for hh in range(heads_per_step):
        c = slice((hh // 2) * pair_w, (hh // 2 + 1) * pair_w)
        q2 = (q_ref[:, c].astype(F32) * LOG2E).astype(BF16)
        lo = dh * (hh % 2)
        qms.append(jnp.where((lane >= lo) & (lane < lo + dh), q2, jnp.zeros_like(q2)))
        biases.append(bias_ref[heads_per_step * hg + hh] * LOG2E)
        cols.append(c)

    def tiles(start, carry, masked):
        out = []
        for hh in range(heads_per_step):
            run, acc = carry[hh]
            kb = k_ref[pl.ds(start, tq), cols[hh]]
            vb = v_ref[pl.ds(start, tq), cols[hh]]
            y = lax.dot_general(qms[hh], kb, (((1,), (1,)), ((), ())), preferred_element_type=F32) + biases[hh]
            ls = jnp.minimum(y, 0.0) - jnp.log2(1.0 + jnp.exp2(_neg_abs(y)))
            lr = ls - y
            if masked:
                lr = jnp.where(causal, lr, 0.0)
            aft = jnp.dot(lr.astype(BF16), tri, preferred_element_type=F32)
            w = jnp.exp2(ls + (aft + run))
            if masked:
                w = jnp.where(causal, w, 0.0)
            acc = acc + jnp.dot(w.astype(BF16), vb, preferred_element_type=F32)
            run = run + aft[:, 0:1] + lr[:, 0:1]
            out.append((run, acc))
        return tuple(out)

    carry = tuple((jnp.zeros((tq, 1), F32), jnp.zeros((tq, pair_w), F32)) for _ in range(heads_per_step))
    carry = tiles(pl.multiple_of(qi * tq, tq), carry, True)
    carry = lax.fori_loop(0, qi, lambda jj, c: tiles(pl.multiple_of((qi - 1 - jj) * tq, tq), c, False), carry)
    for p in range(heads_per_step // 2):
        o_ref[:, cols[2 * p]] = jnp.where(lane < dh, carry[2 * p][1], carry[2 * p + 1][1]).astype(o_ref.dtype)


def _strict_lower_ones(n, dtype):
    s = lax.broadcasted_iota(jnp.int32, (n, n), 0)
    j = lax.broadcasted_iota(jnp.int32, (n, n), 1)
    return (s > j).astype(dtype)


def _sb_prompt(qs, kb, vb, sb_bias, o_init, *, batch, seq, heads, dh, tq):
    nt, wa = qs.shape
    nq = seq // tq
    tri = _strict_lower_ones(tq, BF16)
    hps = SB_HEADS_PER_STEP
    grid_spec = pltpu.PrefetchScalarGridSpec(
        num_scalar_prefetch=1, grid=(batch, heads // hps, nq),
        in_specs=[
            pl.BlockSpec((tq, hps * dh), lambda b, h, i, *_: (b * nq + i, h)),
            pl.BlockSpec((seq, hps * dh), lambda b, h, i, *_: (b, h)),
            pl.BlockSpec((seq, hps * dh), lambda b, h, i, *_: (b, h)),
            pl.BlockSpec((tq, tq), lambda b, h, i, *_: (0, 0)),
            pl.BlockSpec(memory_space=pl.ANY),
        ],
        out_specs=pl.BlockSpec((tq, hps * dh), lambda b, h, i, *_: (b * nq + i, h)),
    )
    return pl.pallas_call(
        functools.partial(_sb_prompt_kernel, tq=tq, dh=dh, heads_per_step=hps), grid_spec=grid_spec,
        out_shape=jax.ShapeDtypeStruct((nt, wa), BF16), input_output_aliases={5: 0},
        compiler_params=_params(("parallel", "parallel", "arbitrary")), name="sb_prompt")(
            sb_bias, qs, kb, vb, tri, o_init)


def _sb_sample_kernel(pt_ref, bias_ref, qm_ref, kn_ref, vn_ref, hm_ref, tri_ref, *refs,
                      pages_per_step, page, past):
    k_refs = refs[:pages_per_step]
    v_refs = refs[pages_per_step:2 * pages_per_step]
    o_ref, run_ref, acc_ref = refs[2 * pages_per_step:]
    del pt_ref
    b = pl.program_id(0)
    g = pl.program_id(1)
    qm = qm_ref[0]
    bias = bias_ref[...]
    qmb = qm.astype(BF16)
    tri = tri_ref[...]

    @pl.when(g == 0)
    def _():
        z = jnp.sum(qm * kn_ref[0], axis=-1, keepdims=True) + bias
        visible = jnp.full(z.shape, past, jnp.int32) < jnp.full(z.shape, past, jnp.int32)
        lr = jnp.where(visible, _neg_softplus(z), 0.0)
        w = jnp.where(visible, jnp.exp(z + lr), 0.0)
        run_ref[...] = lr
        acc_ref[...] = w * vn_ref[0]

    run = run_ref[...]
    acc = acc_ref[...]
    for s in range(pages_per_step):
        kb = k_refs[s][0, 0].astype(BF16)
        vb = v_refs[s][0, 0].astype(BF16)
        z = lax.dot_general(qmb, kb, (((1,), (1,)), ((), ())), preferred_element_type=F32) + bias
        lr = _neg_softplus(z)
        hi, lo = _split_bf16(lr)
        aft = (jnp.dot(hi, tri, preferred_element_type=F32)
               + jnp.dot(lo, tri, preferred_element_type=F32))
        w = jnp.exp(z + lr + aft + run)
        acc = acc + jnp.dot(w.astype(BF16), vb, preferred_element_type=F32)
        run = run + aft[:, 0:1] + lr[:, 0:1]
    run_ref[...] = run
    acc_ref[...] = acc

    @pl.when(g == pl.num_programs(1) - 1)
    def _():
        o_ref[pl.ds(b, 1), :] = jnp.sum(acc * hm_ref[...], axis=0, keepdims=True)


def _sb_sample(q, k_new, v_new, cache_k, cache_v, page_table, sb_bias, *, layer, heads, dh, pages_per_step):
    s_rows, wa = q.shape
    n_pages = page_table.shape[1]
    page = cache_k.shape[2]
    n_phys = cache_k.shape[1]
    ck = cache_k.reshape(cache_k.shape[0], n_phys, page, wa)
    cv = cache_v.reshape(cache_v.shape[0], n_phys, page, wa)
    hrows = 2 * heads
    hsel = (lax.broadcasted_iota(jnp.int32, (hrows, wa), 1) // dh
            == lax.broadcasted_iota(jnp.int32, (hrows, wa), 0)).astype(F32)
    qm = q[:, None, :] * hsel[None]
    bias = jnp.concatenate([sb_bias, jnp.zeros((hrows - heads,), F32)])[:, None]
    tri = _strict_lower_ones(page, BF16)
    steps = n_pages // pages_per_step

    def page_map(s):
        def index_map(b, g, pt):
            return (layer, pt[b, n_pages - 1 - (g * pages_per_step + s)], 0, 0)
        return index_map

    page_specs = [pl.BlockSpec((1, 1, page, wa), page_map(s)) for s in range(pages_per_step)]
    grid_spec = pltpu.PrefetchScalarGridSpec(
        num_scalar_prefetch=1, grid=(s_rows, steps),
        in_specs=[
            pl.BlockSpec((hrows, 1), lambda b, g, pt: (0, 0)),
            pl.BlockSpec((1, hrows, wa), lambda b, g, pt: (b, 0, 0)),
            pl.BlockSpec((1, 1, wa), lambda b, g, pt: (b, 0, 0)),
            pl.BlockSpec((1, 1, wa), lambda b, g, pt: (b, 0, 0)),
            pl.BlockSpec((hrows, wa), lambda b, g, pt: (0, 0)),
            pl.BlockSpec((page, page), lambda b, g, pt: (0, 0)),
        ] + page_specs + page_specs,
        out_specs=pl.BlockSpec((s_rows, wa), lambda b, g, pt: (0, 0)),
        scratch_shapes=[pltpu.VMEM((hrows, 1), F32), pltpu.VMEM((hrows, wa), F32)],
    )
    return pl.pallas_call(
        functools.partial(_sb_sample_kernel, pages_per_step=pages_per_step, page=page,
                          past=n_pages * page),
        grid_spec=grid_spec, out_shape=jax.ShapeDtypeStruct((s_rows, wa), F32),
        compiler_params=_params(("arbitrary", "arbitrary")), name="sb_sample")(
            page_table, bias, qm, k_new[:, None, :], v_new[:, None, :], hsel, tri,
            *([ck] * pages_per_step), *([cv] * pages_per_step))


def _lru_gates(xc, wa_ref, ba_ref, wx_ref, bx_ref, lam_ref):
    xb = xc.astype(BF16)
    r = _sigmoid(jnp.dot(xb, wa_ref[...], preferred_element_type=F32) + ba_ref[...])
    i = _sigmoid(jnp.dot(xb, wx_ref[...], preferred_element_type=F32) + bx_ref[...])
    log_a = LRU_C * r * _neg_softplus(-lam_ref[...])
    a = jnp.exp(log_a)
    u = jnp.sqrt(-jnp.tanh(log_a) * (a * a + 1.0)) * (i * xc)
    return a, u


def _lru_prompt_kernel(lx_ref, glg_ref, cw_ref, cb_ref, wa_ref, ba_ref, wx_ref, bx_ref, lam_ref, ob_init,
                       ob_ref, h_ref, buf_ref, xpad, a_s, u_s, *, seq, chunk, taps):
    del ob_init
    width = lx_ref.shape[1]
    xpad[0:SUBLANES, :] = jnp.zeros((SUBLANES, width), F32)
    xpad[SUBLANES:, :] = lx_ref[...]
    rowid = lax.broadcasted_iota(jnp.int32, (SUBLANES, width), 0)
    h = jnp.zeros((1, width), F32)
    for c in range(seq // chunk):
        base = SUBLANES + c * chunk
        xc = cb_ref[...] + cw_ref[taps - 1:taps, :] * xpad[base:base + chunk, :]
        for d in range(1, taps):
            xc = xc + cw_ref[taps - 1 - d:taps - d, :] * xpad[base - d:base - d + chunk, :]
        a, u = _lru_gates(xc, wa_ref, ba_ref, wx_ref, bx_ref, lam_ref)
        a_s[...] = a
        u_s[...] = u

        def group(gi, h):
            r0 = pl.multiple_of(gi * SUBLANES, SUBLANES)
            ga = a_s[pl.ds(r0, SUBLANES), :]
            gu = u_s[pl.ds(r0, SUBLANES), :]
            for d in (1, 2, 4):
                keep = rowid >= d
                gu = jnp.where(keep, ga * pltpu.roll(gu, d, 0) + gu, gu)
                ga = jnp.where(keep, ga * pltpu.roll(ga, d, 0), ga)
            hs = ga * h + gu
            u_s[pl.ds(r0, SUBLANES), :] = hs
            return hs[SUBLANES - 1:SUBLANES, :]

        h = lax.fori_loop(0, chunk // SUBLANES, group, h)
        rows = slice(c * chunk, (c + 1) * chunk)
        ob_ref[rows, :] = (u_s[...] * glg_ref[rows, :].astype(F32)).astype(ob_ref.dtype)
    h_ref[0] = h
    buf_ref[0] = lx_ref[seq - (taps - 1):seq, :]


def _lru_prompt(lx, glg, cw, cb, wa_bd, ba, wx_bd, bx, lam, ob_init, *, batch, seq, chunk):
    nt, width = lx.shape
    taps = cw.shape[0]
    consts = [cw, cb, wa_bd, ba, wx_bd, bx, lam]
    return pl.pallas_call(
        functools.partial(_lru_prompt_kernel, seq=seq, chunk=chunk, taps=taps),
        grid=(batch,),
        in_specs=[pl.BlockSpec((seq, width), lambda b: (b, 0)),
                  pl.BlockSpec((seq, width), lambda b: (b, 0))] + [_full_spec(a) for a in consts]
        + [pl.BlockSpec(memory_space=pl.ANY)],
        input_output_aliases={2 + len(consts): 0},
        out_specs=[pl.BlockSpec((seq, width), lambda b: (b, 0)),
                   pl.BlockSpec((1, 1, width), lambda b: (b, 0, 0)),
                   pl.BlockSpec((1, taps - 1, width), lambda b: (b, 0, 0))],
        out_shape=[jax.ShapeDtypeStruct((nt, width), BF16),
                   jax.ShapeDtypeStruct((batch, 1, width), F32),
                   jax.ShapeDtypeStruct((batch, taps - 1, width), F32)],
        scratch_shapes=[pltpu.VMEM((seq + SUBLANES, width), F32),
                        pltpu.VMEM((chunk, width), F32), pltpu.VMEM((chunk, width), F32)],
        compiler_params=_params(("parallel",)), name="lru_prompt")(lx, glg, *consts, ob_init)


def _conf_prompt_kernel(glu_ref, cw_ref, cb_ref, g_ref, b_ref, oc_init, oc_ref, buf_ref, xpad, win, cc_s,
                        *, seq, chunk, taps, pad):
    del oc_init
    width = glu_ref.shape[1]
    xpad[0:pad, :] = jnp.zeros((pad, width), F32)
    xpad[pad:, :] = glu_ref[...]
    lead = pad - (taps - 1)

    def one_chunk(c, carry):
        t0 = pl.multiple_of(c * chunk, chunk)
        win[...] = xpad[pl.ds(t0, chunk + pad), :]
        for l0 in range(0, width, LANES):
            lanes = slice(l0, l0 + LANES)
            acc = cb_ref[:, lanes] + cw_ref[0:1, lanes] * win[lead:lead + chunk, lanes]
            for j in range(1, taps):
                acc = acc + cw_ref[j:j + 1, lanes] * win[lead + j:lead + j + chunk, lanes]
            cc_s[:, lanes] = acc
        y = _layer_norm(cc_s[...], g_ref[...], b_ref[...])
        oc_ref[pl.ds(t0, chunk), :] = (y * _sigmoid(y)).astype(oc_ref.dtype)
        return carry

    lax.fori_loop(0, seq // chunk, one_chunk, 0)
    buf_ref[0] = glu_ref[seq - (taps - 1):seq, :]


def _conf_prompt(glu, cw, cb, g, b, oc_init, *, batch, seq, chunk):
    nt, width = glu.shape
    taps = cw.shape[0]
    pad = -(-(taps - 1) // SUBLANES) * SUBLANES
    consts = [cw, cb, g, b]
    return pl.pallas_call(
        functools.partial(_conf_prompt_kernel, seq=seq, chunk=chunk, taps=taps, pad=pad),
        grid=(batch,),
        in_specs=[pl.BlockSpec((seq, width), lambda b: (b, 0))] + [_full_spec(a) for a in consts]
        + [pl.BlockSpec(memory_space=pl.ANY)],
        input_output_aliases={1 + len(consts): 0},
        out_specs=[pl.BlockSpec((seq, width), lambda b: (b, 0)),
                   pl.BlockSpec((1, taps - 1, width), lambda b: (b, 0, 0))],
        out_shape=[jax.ShapeDtypeStruct((nt, width), BF16),
                   jax.ShapeDtypeStruct((batch, taps - 1, width), F32)],
        scratch_shapes=[pltpu.VMEM((seq + pad, width), F32), pltpu.VMEM((chunk + pad, width), F32),
                        pltpu.VMEM((chunk, width), F32)],
        compiler_params=_params(("parallel",)), name="conf_prompt")(glu, *consts, oc_init)


def _sample_mix_kernel(lx_ref, glg_ref, glu_ref, oas_ref, h0_ref, lbuf_ref, cbuf_ref,
                       lcw_ref, lcb_ref, wa_ref, ba_ref, wx_ref, bx_ref, lam_ref,
                       ccw_ref, ccb_ref, cg_ref, cbeta_ref, oa_in, ob_in, oc_in,
                       oa_ref, ob_ref, oc_ref, h_ref, lbuf_out, cbuf_out, *, ltaps, ctaps):
    del oa_in, ob_in, oc_in
    oa_ref[...] = oas_ref[...].astype(oa_ref.dtype)
    lx = lx_ref[...]
    xc = lcb_ref[...] + lcw_ref[ltaps - 1:ltaps, :] * lx
    for j in range(ltaps - 1):
        xc = xc + lcw_ref[j:j + 1, :] * lbuf_ref[j]
    a, u = _lru_gates(xc, wa_ref, ba_ref, wx_ref, bx_ref, lam_ref)
    h = a * h0_ref[...] + u
    h_ref[...] = h
    ob_ref[...] = (h * glg_ref[...].astype(F32)).astype(ob_ref.dtype)
    for j in range(ltaps - 2):
        lbuf_out[j] = lbuf_ref[j + 1]
    lbuf_out[ltaps - 2] = lx

    glu = glu_ref[...]
    cc = ccb_ref[...] + ccw_ref[ctaps - 1:ctaps, :] * glu
    for j in range(ctaps - 1):
        cc = cc + ccw_ref[j:j + 1, :] * cbuf_ref[j]
    y = _layer_norm(cc, cg_ref[...], cbeta_ref[...])
    oc_ref[...] = (y * _sigmoid(y)).astype(oc_ref.dtype)
    for j in range(ctaps - 2):
        cbuf_out[j] = cbuf_ref[j + 1]
    cbuf_out[ctaps - 2] = glu


def _sample_mix(lx, glg, glu, oa_s, h0, lbuf, cbuf, lru_consts, conf_consts, oa_all, ob_all, oc_all, *, row0, rows):
    width = lx.shape[1]
    blk = row0 // rows
    ltaps = lru_consts[0].shape[0]
    ctaps = conf_consts[0].shape[0]
    lbuf_t = jnp.transpose(lbuf, (1, 0, 2))
    cbuf_t = jnp.transpose(cbuf, (1, 0, 2))
    consts = list(lru_consts) + list(conf_consts)
    tail = pl.BlockSpec((rows, width), lambda i: (blk, 0))
    any_spec = pl.BlockSpec(memory_space=pl.ANY)
    ins = [lx, glg, glu, oa_s, h0, lbuf_t, cbuf_t] + consts + [oa_all, ob_all, oc_all]
    in_specs = ([tail, tail, tail, _full_spec(oa_s), _full_spec(h0), _full_spec(lbuf_t), _full_spec(cbuf_t)]
                + [_full_spec(a) for a in consts] + [any_spec, any_spec, any_spec])
    n_in = len(ins)
    oa, ob, oc, h, lb, cb = pl.pallas_call(
        functools.partial(_sample_mix_kernel, ltaps=ltaps, ctaps=ctaps),
        grid=(1,), in_specs=in_specs,
        out_specs=[tail, tail, tail, _full_spec(h0), _full_spec(lbuf_t), _full_spec(cbuf_t)],
        out_shape=[jax.ShapeDtypeStruct(a.shape, a.dtype) for a in (oa_all, ob_all, oc_all)]
        + [jax.ShapeDtypeStruct(h0.shape, F32), jax.ShapeDtypeStruct(lbuf_t.shape, F32),
           jax.ShapeDtypeStruct(cbuf_t.shape, F32)],
        input_output_aliases={n_in - 3: 0, n_in - 2: 1, n_in - 1: 2},
        compiler_params=_params(("arbitrary",)), name="sample_mix")(*ins)
    return oa, ob, oc, h, jnp.transpose(lb, (1, 0, 2)), jnp.transpose(cb, (1, 0, 2))


def _merge_body(rows, consts, outs, *, alpha, n_experts, d_model):
    x_ref, oa_ref, ob_ref, oc_ref = rows
    wgate_ref, bgate_ref, wbr_ref, wout_ref, g_ref, b_ref, rw_ref, rb_ref = consts
    x1_ref, ridx_ref, rgw_ref = outs
    x = x_ref[...]
    xb = x.astype(BF16)
    m = None
    for i, o_ref in enumerate((oa_ref, ob_ref, oc_ref)):
        cols = slice(i * d_model, (i + 1) * d_model)
        gate = _sigmoid(jnp.dot(xb, wgate_ref[:, cols], preferred_element_type=F32) + bgate_ref[:, cols])
        p = gate * jnp.dot(o_ref[...], wbr_ref[i], preferred_element_type=F32)
        m = p if m is None else m + p
    y = alpha * x + jnp.dot(m.astype(BF16), wout_ref[...], preferred_element_type=F32)
    x1 = _layer_norm(y, g_ref[...], b_ref[...])
    x1_ref[...] = x1
    logits = jnp.dot(x1.astype(BF16), rw_ref[...], preferred_element_type=F32) + rb_ref[...]
    lane = lax.broadcasted_iota(jnp.int32, logits.shape, 1)
    lanef = lane.astype(F32)
    logits = jnp.where(lane < n_experts, logits, NEG_INF)
    idx_out = jnp.zeros(logits.shape, F32)
    val_out = jnp.zeros(logits.shape, F32)
    top = None
    for k in range(TOP_K):
        mx = jnp.max(logits, axis=-1, keepdims=True)
        sel = jnp.min(jnp.where(logits == mx, lanef, float(LANES)), axis=-1, keepdims=True)
        top = mx if top is None else top
        idx_out = jnp.where(lane == k, sel, idx_out)
        val_out = jnp.where(lane == k, jnp.exp(mx - top), val_out)
        logits = jnp.where(lanef == sel, NEG_INF, logits)
    ridx_ref[...] = idx_out.astype(jnp.int32)
    rgw_ref[...] = val_out / jnp.sum(val_out, axis=-1, keepdims=True)


def _for_each_row(tt, fn):
    def body(g, c):
        for u in range(ROW_UNROLL):
            fn(g * ROW_UNROLL + u)
        return c

    lax.fori_loop(0, tt // ROW_UNROLL, body, 0)


def _dispatch_kernel(dest_hbm, x_hbm, xs_in, xs_hbm, dest_s, sem_i, sem_r, *, tt):
    del xs_in
    i = pl.program_id(0)
    cp = pltpu.make_async_copy(dest_hbm.at[i], dest_s, sem_i)
    cp.start()
    cp.wait()
    row0 = i * tt

    def copies(r):
        src = x_hbm.at[pl.ds(row0 + r, 1), :]
        return [pltpu.make_async_copy(src, xs_hbm.at[pl.ds(dest_s[0, r * TOP_K + k], 1), :], sem_r)
                for k in range(TOP_K)]

    _for_each_row(tt, lambda r: [c.start() for c in copies(r)])
    _for_each_row(tt, lambda r: [c.wait() for c in copies(r)])


def _dispatch(x1, dest2, xs_zero, *, tt):
    nt, _ = x1.shape
    any_spec = pl.BlockSpec(memory_space=pl.ANY)
    return pl.pallas_call(
        functools.partial(_dispatch_kernel, tt=tt), grid=(nt // tt,),
        in_specs=[any_spec, any_spec, any_spec], out_specs=any_spec,
        out_shape=jax.ShapeDtypeStruct(xs_zero.shape, xs_zero.dtype),
        scratch_shapes=[pltpu.SMEM((1, tt * TOP_K), jnp.int32), pltpu.SemaphoreType.DMA, pltpu.SemaphoreType.DMA],
        input_output_aliases={2: 0},
        compiler_params=_params(("arbitrary",)), name="moe_dispatch")(dest2, x1, xs_zero)


def _expert_kernel(be_ref, nu_ref, first_ref, x_ref, wg_ref, bg_ref, wu_ref, bu_ref, wd_ref, bd_ref, y_ref,
                   wg_s, wu_s, wd_s):
    i = pl.program_id(0)

    @pl.when(first_ref[i] == 1)
    def _():
        wg_s[...] = wg_ref[0, 0].astype(BF16)
        wu_s[...] = wu_ref[0, 0].astype(BF16)
        wd_s[...] = wd_ref[0, 0].astype(BF16)

    @pl.when(i < nu_ref[0])
    def _():
        xb = x_ref[...].astype(BF16)
        g = jnp.minimum(jnp.dot(xb, wg_s[...], preferred_element_type=F32) + bg_ref[0], SWIGLU_LIMIT)
        u = jnp.clip(jnp.dot(xb, wu_s[...], preferred_element_type=F32) + bu_ref[0], -SWIGLU_LIMIT, SWIGLU_LIMIT)
        h = (u + 1.0) * (g * _sigmoid(SWIGLU_ALPHA * g))
        y_ref[...] = jnp.dot(h.astype(BF16), wd_s[...], preferred_element_type=F32) + bd_ref[0]

    @pl.when(i >= nu_ref[0])
    def _():
        y_ref[...] = jnp.zeros(y_ref.shape, y_ref.dtype)


def _experts(xs, block_e, n_used, first, wg, bg, wu, bu, wd, bd, *, layer):
    rows, d = xs.shape
    dff = wg.shape[3]
    nb = rows // EXPERT_ROWS
    wspec = lambda shape: pl.BlockSpec((1, 1) + shape, lambda i, be, nu, fi: (layer, be[i], 0, 0))
    bspec = lambda n: pl.BlockSpec((1, 1, n), lambda i, be, nu, fi: (be[i], 0, 0))
    grid_spec = pltpu.PrefetchScalarGridSpec(
        num_scalar_prefetch=3, grid=(nb,),
        in_specs=[pl.BlockSpec((EXPERT_ROWS, d), lambda i, be, nu, fi: (i, 0)),
                  wspec((d, dff)), bspec(dff), wspec((d, dff)), bspec(dff), wspec((dff, d)), bspec(d)],
        out_specs=pl.BlockSpec((EXPERT_ROWS, d), lambda i, be, nu, fi: (i, 0)),
        scratch_shapes=[pltpu.VMEM((d, dff), BF16), pltpu.VMEM((d, dff), BF16), pltpu.VMEM((dff, d), BF16)])
    return pl.pallas_call(
        _expert_kernel, grid_spec=grid_spec, out_shape=jax.ShapeDtypeStruct((rows, d), F32),
        compiler_params=_params(("arbitrary",)), name="moe_experts")(
            block_e, n_used, first, xs, wg, bg[:, None, :], wu, bu[:, None, :], wd, bd[:, None, :])


def _combine_kernel(dest_hbm, ys_hbm, x1_ref, gw_ref, g_ref, b_ref, x2_ref, dest_s, ybuf, sem_i, sem_r, *, tt, alpha):
    i = pl.program_id(0)
    slot = i % 2

    def gathers(r, s):
        return [pltpu.make_async_copy(ys_hbm.at[pl.ds(dest_s[s, 0, r * TOP_K + k], 1), :],
                                      ybuf.at[s, k, pl.ds(r, 1), :], sem_r.at[s]) for k in range(TOP_K)]

    def fetch(tile, s):
        cp = pltpu.make_async_copy(dest_hbm.at[tile], dest_s.at[s], sem_i)
        cp.start()
        cp.wait()
        _for_each_row(tt, lambda r: [c.start() for c in gathers(r, s)])

    @pl.when(i == 0)
    def _():
        fetch(0, 0)

    @pl.when(i + 1 < pl.num_programs(0))
    def _():
        fetch(i + 1, 1 - slot)

    _for_each_row(tt, lambda r: [c.wait() for c in gathers(r, slot)])
    gw = gw_ref[...]
    moe = gw[:, 0:1] * ybuf[slot, 0]
    for k in range(1, TOP_K):
        moe = moe + gw[:, k:k + 1] * ybuf[slot, k]
    x2_ref[...] = _layer_norm(alpha * x1_ref[...] + moe, g_ref[...], b_ref[...])


def _combine(ys, dest2, x1, rgw, g, b, *, tt, alpha):
    nt, d = x1.shape
    return pl.pallas_call(
        functools.partial(_combine_kernel, tt=tt, alpha=alpha), grid=(nt // tt,),
        in_specs=[pl.BlockSpec(memory_space=pl.ANY), pl.BlockSpec(memory_space=pl.ANY),
                  pl.BlockSpec((tt, d), lambda i: (i, 0)), pl.BlockSpec((tt, LANES), lambda i: (i, 0)),
                  _full_spec(g), _full_spec(b)],
        out_specs=pl.BlockSpec((tt, d), lambda i: (i, 0)),
        out_shape=jax.ShapeDtypeStruct((nt, d), F32),
        scratch_shapes=[pltpu.SMEM((2, 1, tt * TOP_K), jnp.int32), pltpu.VMEM((2, TOP_K, tt, d), F32),
                        pltpu.SemaphoreType.DMA, pltpu.SemaphoreType.DMA((2,))],
        compiler_params=_params(("arbitrary",)), name="moe_combine")(dest2, ys, x1, rgw, g, b)


def _route_rank_kernel(ridx_ref, tri_ref, rank_ref, cnt_ref, carry):
    @pl.when(pl.program_id(0) == 0)
    def _():
        carry[...] = jnp.zeros(carry.shape, F32)

    idx = ridx_ref[...]
    lane = lax.broadcasted_iota(jnp.int32, idx.shape, 1)
    onehot = jnp.zeros(idx.shape, F32)
    for k in range(TOP_K):
        onehot = onehot + (idx[:, k:k + 1] == lane).astype(F32)
    rank_ref[...] = jnp.dot(tri_ref[...], onehot.astype(BF16), preferred_element_type=F32) + carry[...]
    carry[...] = carry[...] + jnp.sum(onehot, axis=0, keepdims=True)
    cnt_ref[...] = carry[...]


def _route(ridx, n_experts, tt):
    nt = ridx.shape[0]
    n_assign = nt * TOP_K
    tri = _strict_lower_ones(tt, BF16)
    rank, cnt = pl.pallas_call(
        _route_rank_kernel, grid=(nt // tt,),
        in_specs=[pl.BlockSpec((tt, LANES), lambda i: (i, 0)), _full_spec(tri)],
        out_specs=[pl.BlockSpec((tt, LANES), lambda i: (i, 0)), pl.BlockSpec((1, LANES), lambda i: (0, 0))],
        out_shape=[jax.ShapeDtypeStruct((nt, LANES), F32), jax.ShapeDtypeStruct((1, LANES), F32)],
        scratch_shapes=[pltpu.VMEM((1, LANES), F32)],
        compiler_params=_params(("arbitrary",)), name="route_rank")(ridx, tri)
    counts = cnt[0, :n_experts].astype(jnp.int32)
    padded = (counts + EXPERT_ROWS - 1) // EXPERT_ROWS * EXPERT_ROWS
    pend = jnp.cumsum(padded)
    pstart = pend - padded
    slot = rank[:, :n_experts].astype(jnp.int32) + pstart[None, :]
    dest = jnp.take_along_axis(slot, ridx[:, :TOP_K], axis=1)
    n_blocks = -(-n_assign // EXPERT_ROWS) + n_experts
    block_e = jnp.clip(jnp.searchsorted(pend, jnp.arange(n_blocks, dtype=jnp.int32) * EXPERT_ROWS, side="right"),
                       0, n_experts - 1).astype(jnp.int32)
    first = jnp.concatenate([jnp.ones((1,), jnp.int32), (block_e[1:] != block_e[:-1]).astype(jnp.int32)])
    n_used = (pend[-1:] // EXPERT_ROWS).astype(jnp.int32)
    return dest.reshape(nt // tt, 1, tt * TOP_K), block_e, n_used, first, n_blocks


def _block_diag(w):
    nb, bi, bo = w.shape
    eye = jnp.eye(nb, dtype=w.dtype)
    return (w[:, :, None, :] * eye[:, None, :, None]).reshape(nb * bi, nb * bo)


def _largest_tile(n, limit, align):
    best = align
    for t in range(align, limit + 1, align):
        if n % t == 0:
            best = t
    return best


def kernel(x_prompt, x_sample, cache_k, cache_v, state_lru_h, state_lru_conv, state_conf_conv, page_table, w_in, b_merge, sb_bias, lru_conv_w, lru_conv_b, lru_wa, lru_ba, lru_wx, lru_bx, lru_lambda, conf_conv_w, conf_conv_b, conf_ln_g, conf_ln_b, w_branch, w_out, ln1_g, ln1_b, router_w, router_b, moe_w_gate, moe_b_gate, moe_w_up, moe_b_up, moe_w_down, moe_b_down, ln2_g, ln2_b):
    batch, seq, d_model = x_prompt.shape
    s_rows = x_sample.shape[0]
    depth = w_in.shape[0]
    heads, dh = cache_k.shape[3], cache_k.shape[4]
    wa = heads * dh
    wb = lru_conv_w.shape[2]
    wc = conf_conv_w.shape[2]
    n_experts = router_w.shape[2]
    n_prompt = batch * seq
    nt = n_prompt + s_rows
    alpha = float((2 * depth) ** 0.25)
    tm = _largest_tile(nt, 640, 16)
    tt = _largest_tile(nt, 320, 8)
    row = lambda v: v[None, :]

    x = jnp.concatenate([x_prompt.reshape(n_prompt, d_model), x_sample.reshape(s_rows, d_model)], axis=0)
    outs = {k: [] for k in ("kp", "vp", "ks", "vs", "hp", "hs", "lp", "ls", "cp", "cs")}
    for l in range(depth):
        w_l = w_in[l].astype(BF16)
        c0, c1, c2 = 3 * wa, 3 * wa + 2 * wb + 2 * wc, w_in.shape[2]
        qs, k, v, kb, vb = _tokenwise_call(
            functools.partial(_proj_qkv_body, wa=wa, scale=dh ** -0.5), nt, tm, [x], [w_l[:, :c0]],
            [(wa, BF16), (wa, F32), (wa, F32), (wa, BF16), (wa, BF16)], "proj_qkv")
        lx, glg, glu = _tokenwise_call(
            functools.partial(_proj_mix_body, wb=wb, wc=wc), nt, tm, [x], [w_l[:, c0:c1]],
            [(wb, F32), (wb, F32), (wc, F32)], "proj_mix")

        o_a = _sb_prompt(qs, kb, vb, sb_bias[l], jnp.zeros((nt, wa), BF16), batch=batch, seq=seq, heads=heads,
                         dh=dh, tq=256)
        q_s = qs[n_prompt:].astype(F32)
        oa_s = _sb_sample(q_s, k[n_prompt:], v[n_prompt:], cache_k, cache_v, page_table, sb_bias[l],
                          layer=l, heads=heads, dh=dh, pages_per_step=8)

        lru_consts = [lru_conv_w[l], row(lru_conv_b[l]), _block_diag(lru_wa[l]).astype(BF16), row(lru_ba[l]),
                      _block_diag(lru_wx[l]).astype(BF16), row(lru_bx[l]), row(lru_lambda[l])]
        conf_consts = [conf_conv_w[l], row(conf_conv_b[l]), row(conf_ln_g[l]), row(conf_ln_b[l])]
        o_b, h_p, lbuf_p = _lru_prompt(lx, glg, *lru_consts, jnp.zeros((nt, wb), BF16), batch=batch, seq=seq,
                                       chunk=256)
        o_c, cbuf_p = _conf_prompt(glu, *conf_consts, jnp.zeros((nt, wc), BF16), batch=batch, seq=seq, chunk=128)
        o_a, o_b, o_c, h_s, lbuf_s, cbuf_s = _sample_mix(
            lx, glg, glu, oa_s, state_lru_h[l], state_lru_conv[l], state_conf_conv[l], lru_consts, conf_consts,
            o_a, o_b, o_c, row0=n_prompt, rows=s_rows)

        rw = jnp.zeros((d_model, LANES), BF16).at[:, :n_experts].set(router_w[l].astype(BF16))
        rb = jnp.zeros((1, LANES), F32).at[0, :n_experts].set(router_b[l])
        x1, ridx, rgw = _tokenwise_call(
            functools.partial(_merge_body, alpha=alpha, n_experts=n_experts, d_model=d_model), nt, tm,
            [x, o_a, o_b, o_c],
            [w_l[:, c1:c2], b_merge[l].reshape(1, c2 - c1), w_branch[l].astype(BF16), w_out[l].astype(BF16),
             row(ln1_g[l]), row(ln1_b[l]), rw, rb],
            [(d_model, F32), (LANES, jnp.int32), (LANES, F32)], "merge")

        dest2, block_e, n_used, first, n_blocks = _route(ridx, n_experts, tt)
        xs = _dispatch(x1, dest2, jnp.zeros((n_blocks * EXPERT_ROWS, d_model), F32), tt=tt)
        ys = _experts(xs, block_e, n_used, first, moe_w_gate, moe_b_gate[l], moe_w_up, moe_b_up[l],
                      moe_w_down, moe_b_down[l], layer=l)
        x = _combine(ys, dest2, x1, rgw, row(ln2_g[l]), row(ln2_b[l]), tt=tt, alpha=alpha)

        outs["kp"].append(k[:n_prompt].reshape(batch, seq, heads, dh))
        outs["vp"].append(v[:n_prompt].reshape(batch, seq, heads, dh))
        outs["ks"].append(k[n_prompt:].reshape(s_rows, 1, heads, dh))
        outs["vs"].append(v[n_prompt:].reshape(s_rows, 1, heads, dh))
        outs["hp"].append(h_p.reshape(batch, wb))
        outs["hs"].append(h_s)
        outs["lp"].append(lbuf_p)
        outs["ls"].append(lbuf_s)
        outs["cp"].append(cbuf_p)
        outs["cs"].append(cbuf_s)

    st = {k_: jnp.stack(v_) for k_, v_ in outs.items()}
    return (x[:n_prompt].reshape(batch, seq, d_model), x[n_prompt:].reshape(s_rows, 1, d_model),
            st["kp"], st["vp"], st["ks"], st["vs"], st["hp"], st["hs"], st["lp"], st["ls"], st["cp"], st["cs"])
```

```python
import functools

import jax
import jax.numpy as jnp
from jax import lax
from jax.experimental import pallas as pl
from jax.experimental.pallas import tpu as pltpu

F32 = jnp.float32
BF16 = jnp.bfloat16

LN_EPS = 1e-5
LRU_C = 8.0
SWIGLU_LIMIT = 7.0
SWIGLU_ALPHA = 1.702
TOP_K = 4
LANES = 128
SUBLANES = 8
VMEM_LIMIT = 56 * 1024 * 1024
EXPERT_ROWS = 256
SB_HEADS_PER_STEP = 4
ROW_UNROLL = 8
NEG_INF = float("-inf")
LOG2E = 1.4426950408889634


def _sigmoid(x):
    return 1.0 / (1.0 + jnp.exp(-x))


def _neg_softplus(z):
    return -(jnp.maximum(z, 0.0) + jnp.log(1.0 + jnp.exp(-jnp.abs(z))))


def _neg_abs(x):
    bits = lax.bitcast_convert_type(x, jnp.uint32) | jnp.uint32(0x80000000)
    return lax.bitcast_convert_type(bits, F32)


def _layer_norm(y, g, b):
    mu = jnp.mean(y, axis=-1, keepdims=True)
    d = y - mu
    var = jnp.mean(d * d, axis=-1, keepdims=True)
    return d * lax.rsqrt(var + LN_EPS) * g + b


def _split_bf16(x):
    hi = x.astype(BF16)
    lo = (x - hi.astype(F32)).astype(BF16)
    return hi, lo


def _params(sem=None):
    return pltpu.CompilerParams(dimension_semantics=sem, vmem_limit_bytes=VMEM_LIMIT)


def _full_spec(a):
    nd = a.ndim
    return pl.BlockSpec(a.shape, lambda *_: (0,) * nd)


def _tokenwise_call(body, n_rows, tm, row_ins, const_ins, outs, name):
    in_specs = [pl.BlockSpec((tm, a.shape[1]), lambda i: (i, 0)) for a in row_ins]
    in_specs += [_full_spec(a) for a in const_ins]
    out_shape = [jax.ShapeDtypeStruct((n_rows, c), dt) for c, dt in outs]
    out_specs = [pl.BlockSpec((tm, c), lambda i: (i, 0)) for c, _ in outs]
    nr, nc = len(row_ins), len(const_ins)

    def kernel(*refs):
        body(refs[:nr], refs[nr:nr + nc], refs[nr + nc:])

    return pl.pallas_call(
        kernel, grid=(n_rows // tm,), in_specs=in_specs, out_specs=out_specs, out_shape=out_shape,
        compiler_params=_params(("parallel",)), name=name)(*row_ins, *const_ins)


def _proj_qkv_body(rows, consts, outs, *, wa, scale):
    (x_ref,), (w_ref,) = rows, consts
    q_ref, k_ref, v_ref, kb_ref, vb_ref = outs
    acc = jnp.dot(x_ref[...].astype(BF16), w_ref[...], preferred_element_type=F32)
    q_ref[...] = (acc[:, :wa] * scale).astype(BF16)
    k = acc[:, wa:2 * wa]
    v = acc[:, 2 * wa:3 * wa]
    k_ref[...] = k
    v_ref[...] = v
    kb_ref[...] = k.astype(BF16)
    vb_ref[...] = v.astype(BF16)


def _proj_mix_body(rows, consts, outs, *, wb, wc):
    (x_ref,), (w_ref,) = rows, consts
    lx_ref, glg_ref, glu_ref = outs
    acc = jnp.dot(x_ref[...].astype(BF16), w_ref[...], preferred_element_type=F32)
    lx_ref[...] = acc[:, :wb]
    glg_ref[...] = jax.nn.gelu(acc[:, wb:2 * wb])
    glu_ref[...] = acc[:, 2 * wb:2 * wb + wc] * _sigmoid(acc[:, 2 * wb + wc:])


def _sb_prompt_kernel(bias_ref, q_ref, k_ref, v_ref, tri_ref, o_init, o_ref, *, tq, tk, dh, heads_per_step):
    del o_init
    hg = pl.program_id(1)
    qi = pl.program_id(2)
    pair_w = 2 * dh
    lane = lax.broadcasted_iota(jnp.int32, (1, pair_w), 1)
    tri = tri_ref[...]
    col_minus_row = lax.broadcasted_iota(jnp.int32, (tq, tk), 1) - lax.broadcasted_iota(jnp.int32, (tq, tk), 0)
    q_start = qi * tq
    qms, biases, cols = [], [], []
    for hh in range(heads_per_step):
        c = slice((hh // 2) * pair_w, (hh // 2 + 1) * pair_w)
        q2 = (q_ref[:, c].astype(F32) * LOG2E).astype(BF16)
        lo = dh * (hh % 2)
        qms.append(jnp.where((lane >= lo) & (lane < lo + dh), q2, jnp.zeros_like(q2)))
        biases.append(bias_ref[heads_per_step * hg + hh] * LOG2E)
        cols.append(c)

    def tiles(start, carry, masked):
        out = []
        if masked:
            causal = col_minus_row < q_start - start
        for hh in range(heads_per_step):
            run, acc = carry[hh]
            kb = k_ref[pl.ds(start, tk), cols[hh]]
            vb = v_ref[pl.ds(start, tk), cols[hh]]
            y = lax.dot_general(qms[hh], kb, (((1,), (1,)), ((), ())), preferred_element_type=F32) + biases[hh]
            ls = jnp.minimum(y, 0.0) - jnp.log2(1.0 + jnp.exp2(_neg_abs(y)))
            lr = ls - y
            if masked:
                lr = jnp.where(causal, lr, 0.0)
            aft = jnp.dot(lr.astype(BF16), tri, preferred_element_type=F32)
            w = jnp.exp2(ls + (aft + run))
            if masked:
                w = jnp.where(causal, w, 0.0)
            acc = acc + jnp.dot(w.astype(BF16), vb, preferred_element_type=F32)
            run = run + aft[:, 0:1] + lr[:, 0:1]
            out.append((run, acc))
        return tuple(out)

    carry = tuple((jnp.zeros((tq, 1), F32), jnp.zeros((tq, pair_w), F32)) for _ in range(heads_per_step))
    per_q = tq // tk
    for d in range(per_q):
        carry = tiles(pl.multiple_of(q_start + (per_q - 1 - d) * tk, tk), carry, True)
    n_before = qi * per_q
    carry = lax.fori_loop(0, n_before, lambda jj, c: tiles(pl.multiple_of((n_before - 1 - jj) * tk, tk), c, False),
                          carry)
    for p in range(heads_per_step // 2):
        o_ref[:, cols[2 * p]] = jnp.where(lane < dh, carry[2 * p][1], carry[2 * p + 1][1]).astype(o_ref.dtype)


def _strict_lower_ones(n, dtype):
    s = lax.broadcasted_iota(jnp.int32, (n, n), 0)
    j = lax.broadcasted_iota(jnp.int32, (n, n), 1)
    return (s > j).astype(dtype)


def _sb_prompt(qs, kb, vb, sb_bias, o_init, *, batch, seq, heads, dh, tq, tk):
    nt, wa = qs.shape
    nq = seq // tq
    tri = _strict_lower_ones(tk, BF16)
    hps = SB_HEADS_PER_STEP
    grid_spec = pltpu.PrefetchScalarGridSpec(
        num_scalar_prefetch=1, grid=(batch, heads // hps, nq),
        in_specs=[
            pl.BlockSpec((tq, hps * dh), lambda b, h, i, *_: (b * nq + i, h)),
            pl.BlockSpec((seq, hps * dh), lambda b, h, i, *_: (b, h)),
            pl.BlockSpec((seq, hps * dh), lambda b, h, i, *_: (b, h)),
            pl.BlockSpec((tk, tk), lambda b, h, i, *_: (0, 0)),
            pl.BlockSpec(memory_space=pl.ANY),
        ],
        out_specs=pl.BlockSpec((tq, hps * dh), lambda b, h, i, *_: (b * nq + i, h)),
    )
    return pl.pallas_call(
        functools.partial(_sb_prompt_kernel, tq=tq, tk=tk, dh=dh, heads_per_step=hps), grid_spec=grid_spec,
        out_shape=jax.ShapeDtypeStruct((nt, wa), BF16), input_output_aliases={5: 0},
        compiler_params=_params(("parallel", "parallel", "arbitrary")), name="sb_prompt")(
            sb_bias, qs, kb, vb, tri, o_init)


def _sb_sample_kernel(pt_ref, bias_ref, qsel_ref, kn_ref, vn_ref, tri_ref, *refs,
                      pages_per_step, page, heads, past):
    k_refs = refs[:pages_per_step]
    v_refs = refs[pages_per_step:2 * pages_per_step]
    o_ref, run_ref, acc_ref = refs[2 * pages_per_step:]
    del pt_ref
    b = pl.program_id(0)
    g = pl.program_id(1)
    hrows = qsel_ref.shape[2]
    bias = bias_ref[...]
    tri = tri_ref[...]
    nt_dims = (((1,), (1,)), ((), ()))

    @pl.when(g == 0)
    def _():
        q_rows = qsel_ref[0, 0].astype(F32)
        for h in range(1, heads):
            q_rows = q_rows + qsel_ref[0, h].astype(F32)
        z = jnp.sum(q_rows * kn_ref[0], axis=-1, keepdims=True) + bias
        visible = jnp.full(z.shape, past, jnp.int32) < jnp.full(z.shape, past, jnp.int32)
        lr = jnp.where(visible, _neg_softplus(z), 0.0)
        w = jnp.where(visible, jnp.exp(z + lr), 0.0)
        run_ref[...] = lr
        for h in range(heads):
            acc_ref[h] = w * vn_ref[0, h:h + 1, :]

    def head_rows(ref, h):
        return ref[0, 0, pl.ds(h, page, stride=heads), :].astype(BF16)

    zs = []
    for s in range(pages_per_step):
        z = bias
        for h in range(heads):
            z = z + lax.dot_general(qsel_ref[0, h], head_rows(k_refs[s], h), nt_dims, preferred_element_type=F32)
        zs.append(z)
    z = jnp.concatenate(zs, axis=0)
    lr = _neg_softplus(z)
    hi, lo = _split_bf16(lr)
    aft = jnp.dot(hi, tri, preferred_element_type=F32) + jnp.dot(lo, tri, preferred_element_type=F32)
    total = aft[:, 0:1] + lr[:, 0:1]
    run = run_ref[...]
    runs = []
    for s in range(pages_per_step):
        runs.append(run)
        run = run + total[s * hrows:(s + 1) * hrows]
    run_ref[...] = run
    w = jnp.exp(z + lr + aft + jnp.concatenate(runs, axis=0)).astype(BF16)
    for h in range(heads):
        acc = acc_ref[h]
        for s in range(pages_per_step):
            acc = acc + jnp.dot(w[s * hrows:(s + 1) * hrows], head_rows(v_refs[s], h), preferred_element_type=F32)
        acc_ref[h] = acc

    @pl.when(g == pl.num_programs(1) - 1)
    def _():
        for h in range(heads):
            o_ref[pl.ds(b, 1), h:h + 1, :] = acc_ref[h, h:h + 1, :][None]


def _sb_sample(q, k_new, v_new, cache_k, cache_v, page_table, sb_bias, *, layer, heads, dh, pages_per_step):
    s_rows, wa = q.shape
    n_pages = page_table.shape[1]
    page = cache_k.shape[2]
    n_phys = cache_k.shape[1]
    ck = cache_k.reshape(cache_k.shape[0], n_phys, page * heads, dh)
    cv = cache_v.reshape(cache_v.shape[0], n_phys, page * heads, dh)
    hrows = 2 * heads
    own_row = (lax.broadcasted_iota(jnp.int32, (heads, hrows, 1), 1)
               == lax.broadcasted_iota(jnp.int32, (heads, hrows, 1), 0))
    qsel = jnp.where(own_row[None], q.reshape(s_rows, heads, 1, dh), 0.0).astype(BF16)
    k_own = jnp.pad(k_new.reshape(s_rows, heads, dh), ((0, 0), (0, hrows - heads), (0, 0)))
    v_own = v_new.reshape(s_rows, heads, dh)
    bias = jnp.concatenate([sb_bias, jnp.zeros((hrows - heads,), F32)])[:, None]
    tri = _strict_lower_ones(page, BF16)
    steps = n_pages // pages_per_step

    def page_map(s):
        def index_map(b, g, pt):
            return (layer, pt[b, n_pages - 1 - (g * pages_per_step + s)], 0, 0)
        return index_map

    page_specs = [pl.BlockSpec((1, 1, page * heads, dh), page_map(s)) for s in range(pages_per_step)]
    grid_spec = pltpu.PrefetchScalarGridSpec(
        num_scalar_prefetch=1, grid=(s_rows, steps),
        in_specs=[
            pl.BlockSpec((hrows, 1), lambda b, g, pt: (0, 0)),
            pl.BlockSpec((1, heads, hrows, dh), lambda b, g, pt: (b, 0, 0, 0)),
            pl.BlockSpec((1, hrows, dh), lambda b, g, pt: (b, 0, 0)),
            pl.BlockSpec((1, heads, dh), lambda b, g, pt: (b, 0, 0)),
            pl.BlockSpec((page, page), lambda b, g, pt: (0, 0)),
        ] + page_specs + page_specs,
        out_specs=pl.BlockSpec((s_rows, heads, dh), lambda b, g, pt: (0, 0, 0)),
        scratch_shapes=[pltpu.VMEM((hrows, 1), F32), pltpu.VMEM((heads, hrows, dh), F32)],
    )
    out = pl.pallas_call(
        functools.partial(_sb_sample_kernel, pages_per_step=pages_per_step, page=page, heads=heads,
                          past=n_pages * page),
        grid_spec=grid_spec, out_shape=jax.ShapeDtypeStruct((s_rows, heads, dh), F32),
        compiler_params=_params(("arbitrary", "arbitrary")), name="sb_sample")(
            page_table, bias, qsel, k_own, v_own, tri, *([ck] * pages_per_step), *([cv] * pages_per_step))
    return out.reshape(s_rows, wa)


def _lru_gates(xc, wa_ref, ba_ref, wx_ref, bx_ref, lam_ref):
    xb = xc.astype(BF16)
    r = _sigmoid(jnp.dot(xb, wa_ref[...], preferred_element_type=F32) + ba_ref[...])
    i = _sigmoid(jnp.dot(xb, wx_ref[...], preferred_element_type=F32) + bx_ref[...])
    log_a = LRU_C * r * _neg_softplus(-lam_ref[...])
    a = jnp.exp(log_a)
    u = jnp.sqrt(-jnp.tanh(log_a) * (a * a + 1.0)) * (i * xc)
    return a, u


def _lru_prompt_kernel(lx_ref, glg_ref, cw_ref, cb_ref, wa_ref, ba_ref, wx_ref, bx_ref, lam_ref, ob_init,
                       ob_ref, h_ref, buf_ref, xpad, a_s, u_s, *, seq, chunk, taps):
    del ob_init
    width = lx_ref.shape[1]
    xpad[0:SUBLANES, :] = jnp.zeros((SUBLANES, width), F32)
    xpad[SUBLANES:, :] = lx_ref[...]
    rowid = lax.broadcasted_iota(jnp.int32, (SUBLANES, width), 0)
    h = jnp.zeros((1, width), F32)
    for c in range(seq // chunk):
        base = SUBLANES + c * chunk
        xc = cb_ref[...] + cw_ref[taps - 1:taps, :] * xpad[base:base + chunk, :]
        for d in range(1, taps):
            xc = xc + cw_ref[taps - 1 - d:taps - d, :] * xpad[base - d:base - d + chunk, :]
        a, u = _lru_gates(xc, wa_ref, ba_ref, wx_ref, bx_ref, lam_ref)
        a_s[...] = a
        u_s[...] = u

        def group(gi, h):
            r0 = pl.multiple_of(gi * SUBLANES, SUBLANES)
            ga = a_s[pl.ds(r0, SUBLANES), :]
            gu = u_s[pl.ds(r0, SUBLANES), :]
            for d in (1, 2, 4):
                keep = rowid >= d
                gu = jnp.where(keep, ga * pltpu.roll(gu, d, 0) + gu, gu)
                ga = jnp.where(keep, ga * pltpu.roll(ga, d, 0), ga)
            hs = ga * h + gu
            u_s[pl.ds(r0, SUBLANES), :] = hs
            return hs[SUBLANES - 1:SUBLANES, :]

        h = lax.fori_loop(0, chunk // SUBLANES, group, h)
        rows = slice(c * chunk, (c + 1) * chunk)
        ob_ref[rows, :] = (u_s[...] * glg_ref[rows, :].astype(F32)).astype(ob_ref.dtype)
    h_ref[0] = h
    buf_ref[0] = lx_ref[seq - (taps - 1):seq, :]


def _lru_prompt(lx, glg, cw, cb, wa_bd, ba, wx_bd, bx, lam, ob_init, *, batch, seq, chunk):
    nt, width = lx.shape
    taps = cw.shape[0]
    consts = [cw, cb, wa_bd, ba, wx_bd, bx, lam]
    return pl.pallas_call(
        functools.partial(_lru_prompt_kernel, seq=seq, chunk=chunk, taps=taps),
        grid=(batch,),
        in_specs=[pl.BlockSpec((seq, width), lambda b: (b, 0)),
                  pl.BlockSpec((seq, width), lambda b: (b, 0))] + [_full_spec(a) for a in consts]
        + [pl.BlockSpec(memory_space=pl.ANY)],
        input_output_aliases={2 + len(consts): 0},
        out_specs=[pl.BlockSpec((seq, width), lambda b: (b, 0)),
                   pl.BlockSpec((1, 1, width), lambda b: (b, 0, 0)),
                   pl.BlockSpec((1, taps - 1, width), lambda b: (b, 0, 0))],
        out_shape=[jax.ShapeDtypeStruct((nt, width), BF16),
                   jax.ShapeDtypeStruct((batch, 1, width), F32),
                   jax.ShapeDtypeStruct((batch, taps - 1, width), F32)],
        scratch_shapes=[pltpu.VMEM((seq + SUBLANES, width), F32),
                        pltpu.VMEM((chunk, width), F32), pltpu.VMEM((chunk, width), F32)],
        compiler_params=_params(("parallel",)), name="lru_prompt")(lx, glg, *consts, ob_init)


def _conf_prompt_kernel(glu_ref, cw_ref, cb_ref, g_ref, b_ref, oc_init, oc_ref, buf_ref, xpad, win, cc_s,
                        *, seq, chunk, taps, pad):
    del oc_init
    width = glu_ref.shape[1]
    xpad[0:pad, :] = jnp.zeros((pad, width), F32)
    xpad[pad:, :] = glu_ref[...]
    lead = pad - (taps - 1)

    def one_chunk(c, carry):
        t0 = pl.multiple_of(c * chunk, chunk)
        win[...] = xpad[pl.ds(t0, chunk + pad), :]
        for l0 in range(0, width, LANES):
            lanes = slice(l0, l0 + LANES)
            acc = cb_ref[:, lanes] + cw_ref[0:1, lanes] * win[lead:lead + chunk, lanes]
            for j in range(1, taps):
                acc = acc + cw_ref[j:j + 1, lanes] * win[lead + j:lead + j + chunk, lanes]
            cc_s[:, lanes] = acc
        y = _layer_norm(cc_s[...], g_ref[...], b_ref[...])
        oc_ref[pl.ds(t0, chunk), :] = (y * _sigmoid(y)).astype(oc_ref.dtype)
        return carry

    lax.fori_loop(0, seq // chunk, one_chunk, 0)
    buf_ref[0] = glu_ref[seq - (taps - 1):seq, :]


def _conf_prompt(glu, cw, cb, g, b, oc_init, *, batch, seq, chunk):
    nt, width = glu.shape
    taps = cw.shape[0]
    pad = -(-(taps - 1) // SUBLANES) * SUBLANES
    consts = [cw, cb, g, b]
    return pl.pallas_call(
        functools.partial(_conf_prompt_kernel, seq=seq, chunk=chunk, taps=taps, pad=pad),
        grid=(batch,),
        in_specs=[pl.BlockSpec((seq, width), lambda b: (b, 0))] + [_full_spec(a) for a in consts]
        + [pl.BlockSpec(memory_space=pl.ANY)],
        input_output_aliases={1 + len(consts): 0},
        out_specs=[pl.BlockSpec((seq, width), lambda b: (b, 0)),
                   pl.BlockSpec((1, taps - 1, width), lambda b: (b, 0, 0))],
        out_shape=[jax.ShapeDtypeStruct((nt, width), BF16),
                   jax.ShapeDtypeStruct((batch, taps - 1, width), F32)],
        scratch_shapes=[pltpu.VMEM((seq + pad, width), F32), pltpu.VMEM((chunk + pad, width), F32),
                        pltpu.VMEM((chunk, width), F32)],
        compiler_params=_params(("parallel",)), name="conf_prompt")(glu, *consts, oc_init)


def _sample_mix_kernel(lx_ref, glg_ref, glu_ref, oas_ref, h0_ref, lbuf_ref, cbuf_ref,
                       lcw_ref, lcb_ref, wa_ref, ba_ref, wx_ref, bx_ref, lam_ref,
                       ccw_ref, ccb_ref, cg_ref, cbeta_ref, oa_in, ob_in, oc_in,
                       oa_ref, ob_ref, oc_ref, h_ref, lbuf_out, cbuf_out, *, ltaps, ctaps):
    del oa_in, ob_in, oc_in
    oa_ref[...] = oas_ref[...].astype(oa_ref.dtype)
    lx = lx_ref[...]
    xc = lcb_ref[...] + lcw_ref[ltaps - 1:ltaps, :] * lx
    for j in range(ltaps - 1):
        xc = xc + lcw_ref[j:j + 1, :] * lbuf_ref[j]
    a, u = _lru_gates(xc, wa_ref, ba_ref, wx_ref, bx_ref, lam_ref)
    h = a * h0_ref[...] + u
    h_ref[...] = h
    ob_ref[...] = (h * glg_ref[...].astype(F32)).astype(ob_ref.dtype)
    for j in range(ltaps - 2):
        lbuf_out[j] = lbuf_ref[j + 1]
    lbuf_out[ltaps - 2] = lx

    glu = glu_ref[...]
    cc = ccb_ref[...] + ccw_ref[ctaps - 1:ctaps, :] * glu
    for j in range(ctaps - 1):
        cc = cc + ccw_ref[j:j + 1, :] * cbuf_ref[j]
    y = _layer_norm(cc, cg_ref[...], cbeta_ref[...])
    oc_ref[...] = (y * _sigmoid(y)).astype(oc_ref.dtype)
    for j in range(ctaps - 2):
        cbuf_out[j] = cbuf_ref[j + 1]
    cbuf_out[ctaps - 2] = glu


def _sample_mix(lx, glg, glu, oa_s, h0, lbuf, cbuf, lru_consts, conf_consts, oa_all, ob_all, oc_all, *, row0, rows):
    width = lx.shape[1]
    blk = row0 // rows
    ltaps = lru_consts[0].shape[0]
    ctaps = conf_consts[0].shape[0]
    lbuf_t = jnp.transpose(lbuf, (1, 0, 2))
    cbuf_t = jnp.transpose(cbuf, (1, 0, 2))
    consts = list(lru_consts) + list(conf_consts)
    tail = pl.BlockSpec((rows, width), lambda i: (blk, 0))
    any_spec = pl.BlockSpec(memory_space=pl.ANY)
    ins = [lx, glg, glu, oa_s, h0, lbuf_t, cbuf_t] + consts + [oa_all, ob_all, oc_all]
    in_specs = ([tail, tail, tail, _full_spec(oa_s), _full_spec(h0), _full_spec(lbuf_t), _full_spec(cbuf_t)]
                + [_full_spec(a) for a in consts] + [any_spec, any_spec, any_spec])
    n_in = len(ins)
    oa, ob, oc, h, lb, cb = pl.pallas_call(
        functools.partial(_sample_mix_kernel, ltaps=ltaps, ctaps=ctaps),
        grid=(1,), in_specs=in_specs,
        out_specs=[tail, tail, tail, _full_spec(h0), _full_spec(lbuf_t), _full_spec(cbuf_t)],
        out_shape=[jax.ShapeDtypeStruct(a.shape, a.dtype) for a in (oa_all, ob_all, oc_all)]
        + [jax.ShapeDtypeStruct(h0.shape, F32), jax.ShapeDtypeStruct(lbuf_t.shape, F32),
           jax.ShapeDtypeStruct(cbuf_t.shape, F32)],
        input_output_aliases={n_in - 3: 0, n_in - 2: 1, n_in - 1: 2},
        compiler_params=_params(("arbitrary",)), name="sample_mix")(*ins)
    return oa, ob, oc, h, jnp.transpose(lb, (1, 0, 2)), jnp.transpose(cb, (1, 0, 2))


def _merge_body(rows, consts, outs, *, alpha, n_experts, d_model):
    x_ref, oa_ref, ob_ref, oc_ref = rows
    wgate_ref, bgate_ref, wbr_ref, wout_ref, g_ref, b_ref, rw_ref, rb_ref = consts
    x1_ref, ridx_ref, rgw_ref = outs
    x = x_ref[...]
    xb = x.astype(BF16)
    m = None
    for i, o_ref in enumerate((oa_ref, ob_ref, oc_ref)):
        cols = slice(i * d_model, (i + 1) * d_model)
        gate = _sigmoid(jnp.dot(xb, wgate_ref[:, cols], preferred_element_type=F32) + bgate_ref[:, cols])
        p = gate * jnp.dot(o_ref[...], wbr_ref[i], preferred_element_type=F32)
        m = p if m is None else m + p
    y = alpha * x + jnp.dot(m.astype(BF16), wout_ref[...], preferred_element_type=F32)
    x1 = _layer_norm(y, g_ref[...], b_ref[...])
    x1_ref[...] = x1
    logits = jnp.dot(x1.astype(BF16), rw_ref[...], preferred_element_type=F32) + rb_ref[...]
    lane = lax.broadcasted_iota(jnp.int32, logits.shape, 1)
    lanef = lane.astype(F32)
    logits = jnp.where(lane < n_experts, logits, NEG_INF)
    idx_out = jnp.zeros(logits.shape, F32)
    val_out = jnp.zeros(logits.shape, F32)
    top = None
    for k in range(TOP_K):
        mx = jnp.max(logits, axis=-1, keepdims=True)
        sel = jnp.min(jnp.where(logits == mx, lanef, float(LANES)), axis=-1, keepdims=True)
        top = mx if top is None else top
        idx_out = jnp.where(lane == k, sel, idx_out)
        val_out = jnp.where(lane == k, jnp.exp(mx - top), val_out)
        logits = jnp.where(lanef == sel, NEG_INF, logits)
    ridx_ref[...] = idx_out.astype(jnp.int32)
    rgw_ref[...] = val_out / jnp.sum(val_out, axis=-1, keepdims=True)


def _for_each_row(tt, fn):
    def body(g, c):
        for u in range(ROW_UNROLL):
            fn(g * ROW_UNROLL + u)
        return c

    lax.fori_loop(0, tt // ROW_UNROLL, body, 0)


def _dispatch_kernel(dest_hbm, x_ref, xs_in, xs_hbm, dest_s, sem_i, sem_r, *, tt):
    del xs_in
    i = pl.program_id(0)
    cp = pltpu.make_async_copy(dest_hbm.at[i], dest_s, sem_i)
    cp.start()
    cp.wait()

    def copies(r):
        src = x_ref.at[pl.ds(r, 1), :]
        return [pltpu.make_async_copy(src, xs_hbm.at[pl.ds(dest_s[0, r * TOP_K + k], 1), :], sem_r)
                for k in range(TOP_K)]

    _for_each_row(tt, lambda r: [c.start() for c in copies(r)])
    _for_each_row(tt, lambda r: [c.wait() for c in copies(r)])


def _dispatch(x1, dest2, xs_zero, *, tt):
    nt, d = x1.shape
    any_spec = pl.BlockSpec(memory_space=pl.ANY)
    return pl.pallas_call(
        functools.partial(_dispatch_kernel, tt=tt), grid=(nt // tt,),
        in_specs=[any_spec, pl.BlockSpec((tt, d), lambda i: (i, 0)), any_spec], out_specs=any_spec,
        out_shape=jax.ShapeDtypeStruct(xs_zero.shape, xs_zero.dtype),
        scratch_shapes=[pltpu.SMEM((1, tt * TOP_K), jnp.int32), pltpu.SemaphoreType.DMA, pltpu.SemaphoreType.DMA],
        input_output_aliases={2: 0},
        compiler_params=_params(("arbitrary",)), name="moe_dispatch")(dest2, x1, xs_zero)


def _expert_kernel(be_ref, nu_ref, first_ref, x_ref, wg_ref, bg_ref, wu_ref, bu_ref, wd_ref, bd_ref, y_ref,
                   wg_s, wu_s, wd_s):
    i = pl.program_id(0)

    @pl.when(first_ref[i] == 1)
    def _():
        wg_s[...] = wg_ref[0, 0].astype(BF16)
        wu_s[...] = wu_ref[0, 0].astype(BF16)
        wd_s[...] = wd_ref[0, 0].astype(BF16)

    @pl.when(i < nu_ref[0])
    def _():
        xb = x_ref[...].astype(BF16)
        g = jnp.minimum(jnp.dot(xb, wg_s[...], preferred_element_type=F32) + bg_ref[0], SWIGLU_LIMIT)
        u = jnp.clip(jnp.dot(xb, wu_s[...], preferred_element_type=F32) + bu_ref[0], -SWIGLU_LIMIT, SWIGLU_LIMIT)
        h = (u + 1.0) * (g * _sigmoid(SWIGLU_ALPHA * g))
        y_ref[...] = jnp.dot(h.astype(BF16), wd_s[...], preferred_element_type=F32) + bd_ref[0]

    @pl.when(i >= nu_ref[0])
    def _():
        y_ref[...] = jnp.zeros(y_ref.shape, y_ref.dtype)


def _experts(xs, block_e, n_used, first, wg, bg, wu, bu, wd, bd, *, layer):
    rows, d = xs.shape
    dff = wg.shape[3]
    nb = rows // EXPERT_ROWS
    wspec = lambda shape: pl.BlockSpec((1, 1) + shape, lambda i, be, nu, fi: (layer, be[i], 0, 0))
    bspec = lambda n: pl.BlockSpec((1, 1, n), lambda i, be, nu, fi: (be[i], 0, 0))
    grid_spec = pltpu.PrefetchScalarGridSpec(
        num_scalar_prefetch=3, grid=(nb,),
        in_specs=[pl.BlockSpec((EXPERT_ROWS, d), lambda i, be, nu, fi: (i, 0)),
                  wspec((d, dff)), bspec(dff), wspec((d, dff)), bspec(dff), wspec((dff, d)), bspec(d)],
        out_specs=pl.BlockSpec((EXPERT_ROWS, d), lambda i, be, nu, fi: (i, 0)),
        scratch_shapes=[pltpu.VMEM((d, dff), BF16), pltpu.VMEM((d, dff), BF16), pltpu.VMEM((dff, d), BF16)])
    return pl.pallas_call(
        _expert_kernel, grid_spec=grid_spec, out_shape=jax.ShapeDtypeStruct((rows, d), F32),
        compiler_params=_params(("arbitrary",)), name="moe_experts")(
            block_e, n_used, first, xs, wg, bg[:, None, :], wu, bu[:, None, :], wd, bd[:, None, :])


def _combine_kernel(dest_hbm, ys_hbm, x1_ref, gw_ref, g_ref, b_ref, x2_ref, dest_s, ybuf, sem_i, sem_r, *, tt, alpha):
    i = pl.program_id(0)
    slot = i % 2

    def gathers(r, s):
        return [pltpu.make_async_copy(ys_hbm.at[pl.ds(dest_s[s, 0, r * TOP_K + k], 1), :],
                                      ybuf.at[s, k, pl.ds(r, 1), :], sem_r.at[s]) for k in range(TOP_K)]

    def fetch(tile, s):
        cp = pltpu.make_async_copy(dest_hbm.at[tile], dest_s.at[s], sem_i)
        cp.start()
        cp.wait()
        _for_each_row(tt, lambda r: [c.start() for c in gathers(r, s)])

    @pl.when(i == 0)
    def _():
        fetch(0, 0)

    @pl.when(i + 1 < pl.num_programs(0))
    def _():
        fetch(i + 1, 1 - slot)

    _for_each_row(tt, lambda r: [c.wait() for c in gathers(r, slot)])
    gw = gw_ref[...]
    moe = gw[:, 0:1] * ybuf[slot, 0]
    for k in range(1, TOP_K):
        moe = moe + gw[:, k:k + 1] * ybuf[slot, k]
    x2_ref[...] = _layer_norm(alpha * x1_ref[...] + moe, g_ref[...], b_ref[...])


def _combine(ys, dest2, x1, rgw, g, b, *, tt, alpha):
    nt, d = x1.shape
    return pl.pallas_call(
        functools.partial(_combine_kernel, tt=tt, alpha=alpha), grid=(nt // tt,),
        in_specs=[pl.BlockSpec(memory_space=pl.ANY), pl.BlockSpec(memory_space=pl.ANY),
                  pl.BlockSpec((tt, d), lambda i: (i, 0)), pl.BlockSpec((tt, LANES), lambda i: (i, 0)),
                  _full_spec(g), _full_spec(b)],
        out_specs=pl.BlockSpec((tt, d), lambda i: (i, 0)),
        out_shape=jax.ShapeDtypeStruct((nt, d), F32),
        scratch_shapes=[pltpu.SMEM((2, 1, tt * TOP_K), jnp.int32), pltpu.VMEM((2, TOP_K, tt, d), F32),
                        pltpu.SemaphoreType.DMA, pltpu.SemaphoreType.DMA((2,))],
        compiler_params=_params(("arbitrary",)), name="moe_combine")(dest2, ys, x1, rgw, g, b)


def _route_rank_kernel(ridx_ref, tri_ref, rank_ref, cnt_ref, carry):
    @pl.when(pl.program_id(0) == 0)
    def _():
        carry[...] = jnp.zeros(carry.shape, F32)

    idx = ridx_ref[...]
    lane = lax.broadcasted_iota(jnp.int32, idx.shape, 1)
    onehot = jnp.zeros(idx.shape, F32)
    for k in range(TOP_K):
        onehot = onehot + (idx[:, k:k + 1] == lane).astype(F32)
    rank_ref[...] = jnp.dot(tri_ref[...], onehot.astype(BF16), preferred_element_type=F32) + carry[...]
    carry[...] = carry[...] + jnp.sum(onehot, axis=0, keepdims=True)
    cnt_ref[...] = carry[...]


def _route(ridx, n_experts, tt):
    nt = ridx.shape[0]
    n_assign = nt * TOP_K
    tri = _strict_lower_ones(tt, BF16)
    rank, cnt = pl.pallas_call(
        _route_rank_kernel, grid=(nt // tt,),
        in_specs=[pl.BlockSpec((tt, LANES), lambda i: (i, 0)), _full_spec(tri)],
        out_specs=[pl.BlockSpec((tt, LANES), lambda i: (i, 0)), pl.BlockSpec((1, LANES), lambda i: (0, 0))],
        out_shape=[jax.ShapeDtypeStruct((nt, LANES), F32), jax.ShapeDtypeStruct((1, LANES), F32)],
        scratch_shapes=[pltpu.VMEM((1, LANES), F32)],
        compiler_params=_params(("arbitrary",)), name="route_rank")(ridx, tri)
    counts = cnt[0, :n_experts].astype(jnp.int32)
    padded = (counts + EXPERT_ROWS - 1) // EXPERT_ROWS * EXPERT_ROWS
    pend = jnp.cumsum(padded)
    pstart = pend - padded
    slot = rank[:, :n_experts].astype(jnp.int32) + pstart[None, :]
    dest = jnp.take_along_axis(slot, ridx[:, :TOP_K], axis=1)
    n_blocks = -(-n_assign // EXPERT_ROWS) + n_experts
    block_row0 = jnp.arange(n_blocks, dtype=jnp.int32) * EXPERT_ROWS
    block_e = jnp.minimum(jnp.sum((pend[None, :] <= block_row0[:, None]).astype(jnp.int32), axis=1), n_experts - 1)
    first = jnp.concatenate([jnp.ones((1,), jnp.int32), (block_e[1:] != block_e[:-1]).astype(jnp.int32)])
    n_used = (pend[-1:] // EXPERT_ROWS).astype(jnp.int32)
    return dest.reshape(nt // tt, 1, tt * TOP_K), block_e, n_used, first, n_blocks


def _block_diag(w):
    nb, bi, bo = w.shape
    eye = jnp.eye(nb, dtype=w.dtype)
    return (w[:, :, None, :] * eye[:, None, :, None]).reshape(nb * bi, nb * bo)


def _largest_tile(n, limit, align):
    best = align
    for t in range(align, limit + 1, align):
        if n % t == 0:
            best = t
    return best


def kernel(x_prompt, x_sample, cache_k, cache_v, state_lru_h, state_lru_conv, state_conf_conv, page_table, w_in, b_merge, sb_bias, lru_conv_w, lru_conv_b, lru_wa, lru_ba, lru_wx, lru_bx, lru_lambda, conf_conv_w, conf_conv_b, conf_ln_g, conf_ln_b, w_branch, w_out, ln1_g, ln1_b, router_w, router_b, moe_w_gate, moe_b_gate, moe_w_up, moe_b_up, moe_w_down, moe_b_down, ln2_g, ln2_b):
    batch, seq, d_model = x_prompt.shape
    s_rows = x_sample.shape[0]
    depth = w_in.shape[0]
    heads, dh = cache_k.shape[3], cache_k.shape[4]
    wa = heads * dh
    wb = lru_conv_w.shape[2]
    wc = conf_conv_w.shape[2]
    n_experts = router_w.shape[2]
    n_prompt = batch * seq
    nt = n_prompt + s_rows
    alpha = float((2 * depth) ** 0.25)
    tm = _largest_tile(nt, 640, 16)
    tt = _largest_tile(nt, 320, 8)
    row = lambda v: v[None, :]

    x = jnp.concatenate([x_prompt.reshape(n_prompt, d_model), x_sample.reshape(s_rows, d_model)], axis=0)
    outs = {k: [] for k in ("kp", "vp", "ks", "vs", "hp", "hs", "lp", "ls", "cp", "cs")}
    for l in range(depth):
        w_l = w_in[l].astype(BF16)
        c0, c1, c2 = 3 * wa, 3 * wa + 2 * wb + 2 * wc, w_in.shape[2]
        qs, k, v, kb, vb = _tokenwise_call(
            functools.partial(_proj_qkv_body, wa=wa, scale=dh ** -0.5), nt, tm, [x], [w_l[:, :c0]],
            [(wa, BF16), (wa, F32), (wa, F32), (wa, BF16), (wa, BF16)], "proj_qkv")
        lx, glg, glu = _tokenwise_call(
            functools.partial(_proj_mix_body, wb=wb, wc=wc), nt, tm, [x], [w_l[:, c0:c1]],
            [(wb, F32), (wb, F32), (wc, F32)], "proj_mix")

        o_a = _sb_prompt(qs, kb, vb, sb_bias[l], jnp.zeros((nt, wa), BF16), batch=batch, seq=seq, heads=heads,
                         dh=dh, tq=512, tk=512)
        q_s = qs[n_prompt:].astype(F32)
        oa_s = _sb_sample(q_s, k[n_prompt:], v[n_prompt:], cache_k, cache_v, page_table, sb_bias[l],
                          layer=l, heads=heads, dh=dh, pages_per_step=8)

        lru_consts = [lru_conv_w[l], row(lru_conv_b[l]), _block_diag(lru_wa[l]).astype(BF16), row(lru_ba[l]),
                      _block_diag(lru_wx[l]).astype(BF16), row(lru_bx[l]), row(lru_lambda[l])]
        conf_consts = [conf_conv_w[l], row(conf_conv_b[l]), row(conf_ln_g[l]), row(conf_ln_b[l])]
        o_b, h_p, lbuf_p = _lru_prompt(lx, glg, *lru_consts, jnp.zeros((nt, wb), BF16), batch=batch, seq=seq,
                                       chunk=256)
        o_c, cbuf_p = _conf_prompt(glu, *conf_consts, jnp.zeros((nt, wc), BF16), batch=batch, seq=seq, chunk=128)
        o_a, o_b, o_c, h_s, lbuf_s, cbuf_s = _sample_mix(
            lx, glg, glu, oa_s, state_lru_h[l], state_lru_conv[l], state_conf_conv[l], lru_consts, conf_consts,
            o_a, o_b, o_c, row0=n_prompt, rows=s_rows)

        rw = jnp.zeros((d_model, LANES), BF16).at[:, :n_experts].set(router_w[l].astype(BF16))
        rb = jnp.zeros((1, LANES), F32).at[0, :n_experts].set(router_b[l])
        x1, ridx, rgw = _tokenwise_call(
            functools.partial(_merge_body, alpha=alpha, n_experts=n_experts, d_model=d_model), nt, tm,
            [x, o_a, o_b, o_c],
            [w_l[:, c1:c2], b_merge[l].reshape(1, c2 - c1), w_branch[l].astype(BF16), w_out[l].astype(BF16),
             row(ln1_g[l]), row(ln1_b[l]), rw, rb],
            [(d_model, F32), (LANES, jnp.int32), (LANES, F32)], "merge")

        dest2, block_e, n_used, first, n_blocks = _route(ridx, n_experts, tt)
        xs = _dispatch(x1, dest2, jnp.zeros((n_blocks * EXPERT_ROWS, d_model), F32), tt=tt)
        ys = _experts(xs, block_e, n_used, first, moe_w_gate, moe_b_gate[l], moe_w_up, moe_b_up[l],
                      moe_w_down, moe_b_down[l], layer=l)
        x = _combine(ys, dest2, x1, rgw, row(ln2_g[l]), row(ln2_b[l]), tt=tt, alpha=alpha)

        outs["kp"].append(k[:n_prompt].reshape(batch, seq, heads, dh))
        outs["vp"].append(v[:n_prompt].reshape(batch, seq, heads, dh))
        outs["ks"].append(k[n_prompt:].reshape(s_rows, 1, heads, dh))
        outs["vs"].append(v[n_prompt:].reshape(s_rows, 1, heads, dh))
        outs["hp"].append(h_p.reshape(batch, wb))
        outs["hs"].append(h_s)
        outs["lp"].append(lbuf_p)
        outs["ls"].append(lbuf_s)
        outs["cp"].append(cbuf_p)
        outs["cs"].append(cbuf_s)

    st = {k_: jnp.stack(v_) for k_, v_ in outs.items()}
    return (x[:n_prompt].reshape(batch, seq, d_model), x[n_prompt:].reshape(s_rows, 1, d_model),
            st["kp"], st["vp"], st["ks"], st["vs"], st["hp"], st["hs"], st["lp"], st["ls"], st["cp"], st["cs"])
```

```python
import functools

import jax
import jax.numpy as jnp
from jax import lax
from jax.experimental import pallas as pl
from jax.experimental.pallas import tpu as pltpu

F32 = jnp.float32
BF16 = jnp.bfloat16

LN_EPS = 1e-5
LRU_C = 8.0
SWIGLU_LIMIT = 7.0
SWIGLU_ALPHA = 1.702
TOP_K = 4
LANES = 128
SUBLANES = 8
VMEM_LIMIT = 56 * 1024 * 1024
EXPERT_ROWS = 512
SB_HEADS_PER_STEP = 4
ROW_UNROLL = 8
NEG_INF = float("-inf")
LOG2E = 1.4426950408889634


def _sigmoid(x):
    return 1.0 / (1.0 + jnp.exp(-x))


def _neg_softplus(z):
    return -(jnp.maximum(z, 0.0) + jnp.log(1.0 + jnp.exp(-jnp.abs(z))))


def _neg_abs(x):
    bits = lax.bitcast_convert_type(x, jnp.uint32) | jnp.uint32(0x80000000)
    return lax.bitcast_convert_type(bits, F32)


def _layer_norm(y, g, b):
    mu = jnp.mean(y, axis=-1, keepdims=True)
    d = y - mu
    var = jnp.mean(d * d, axis=-1, keepdims=True)
    return d * lax.rsqrt(var + LN_EPS) * g + b


def _split_bf16(x):
    hi = x.astype(BF16)
    lo = (x - hi.astype(F32)).astype(BF16)
    return hi, lo


def _params(sem=None):
    return pltpu.CompilerParams(dimension_semantics=sem, vmem_limit_bytes=VMEM_LIMIT)


def _full_spec(a):
    nd = a.ndim
    return pl.BlockSpec(a.shape, lambda *_: (0,) * nd)


def _tokenwise_call(body, n_rows, tm, row_ins, const_ins, outs, name):
    in_specs = [pl.BlockSpec((tm, a.shape[1]), lambda i: (i, 0)) for a in row_ins]
    in_specs += [_full_spec(a) for a in const_ins]
    out_shape = [jax.ShapeDtypeStruct((n_rows, c), dt) for c, dt in outs]
    out_specs = [pl.BlockSpec((tm, c), lambda i: (i, 0)) for c, _ in outs]
    nr, nc = len(row_ins), len(const_ins)

    def kernel(*refs):
        body(refs[:nr], refs[nr:nr + nc], refs[nr + nc:])

    return pl.pallas_call(
        kernel, grid=(n_rows // tm,), in_specs=in_specs, out_specs=out_specs, out_shape=out_shape,
        compiler_params=_params(("parallel",)), name=name)(*row_ins, *const_ins)


def _proj_qkv_body(rows, consts, outs, *, wa, scale):
    (x_ref,), (w_ref,) = rows, consts
    q_ref, k_ref, v_ref, kb_ref, vb_ref = outs
    acc = jnp.dot(x_ref[...].astype(BF16), w_ref[...], preferred_element_type=F32)
    q_ref[...] = (acc[:, :wa] * scale).astype(BF16)
    k = acc[:, wa:2 * wa]
    v = acc[:, 2 * wa:3 * wa]
    k_ref[...] = k
    v_ref[...] = v
    kb_ref[...] = k.astype(BF16)
    vb_ref[...] = v.astype(BF16)


def _proj_mix_body(rows, consts, outs, *, wb, wc):
    (x_ref,), (w_ref,) = rows, consts
    lx_ref, glg_ref, glu_ref = outs
    acc = jnp.dot(x_ref[...].astype(BF16), w_ref[...], preferred_element_type=F32)
    lx_ref[...] = acc[:, :wb]
    glg_ref[...] = jax.nn.gelu(acc[:, wb:2 * wb])
    glu_ref[...] = acc[:, 2 * wb:2 * wb + wc] * _sigmoid(acc[:, 2 * wb + wc:])


def _sb_prompt_kernel(bias_ref, q_ref, k_ref, v_ref, tri_ref, o_init, o_ref, *, tq, tk, dh, heads_per_step):
    del o_init
    hg = pl.program_id(1)
    qi = pl.program_id(2)
    pair_w = 2 * dh
    lane = lax.broadcasted_iota(jnp.int32, (1, pair_w), 1)
    tri = tri_ref[...]
    col_minus_row = lax.broadcasted_iota(jnp.int32, (tq, tk), 1) - lax.broadcasted_iota(jnp.int32, (tq, tk), 0)
    q_start = qi * tq
    qms, biases, cols = [], [], []
    for hh in range(heads_per_step):
        c = slice((hh // 2) * pair_w, (hh // 2 + 1) * pair_w)
        q2 = (q_ref[:, c].astype(F32) * LOG2E).astype(BF16)
        lo = dh * (hh % 2)
        qms.append(jnp.where((lane >= lo) & (lane < lo + dh), q2, jnp.zeros_like(q2)))
        biases.append(bias_ref[heads_per_step * hg + hh] * LOG2E)
        cols.append(c)

    def tiles(start, carry, masked):
        out = []
        if masked:
            causal = col_minus_row < q_start - start
        for hh in range(heads_per_step):
            run, acc = carry[hh]
            kb = k_ref[pl.ds(start, tk), cols[hh]]
            vb = v_ref[pl.ds(start, tk), cols[hh]]
            y = lax.dot_general(qms[hh], kb, (((1,), (1,)), ((), ())), preferred_element_type=F32) + biases[hh]
            ls = jnp.minimum(y, 0.0) - jnp.log2(1.0 + jnp.exp2(_neg_abs(y)))
            lr = ls - y
            if masked:
                lr = jnp.where(causal, lr, 0.0)
            aft = jnp.dot(lr.astype(BF16), tri, preferred_element_type=F32)
            w = jnp.exp2(ls + (aft + run))
            if masked:
                w = jnp.where(causal, w, 0.0)
            acc = acc + jnp.dot(w.astype(BF16), vb, preferred_element_type=F32)
            run = run + aft[:, 0:1] + lr[:, 0:1]
            out.append((run, acc))
        return tuple(out)

    carry = tuple((jnp.zeros((tq, 1), F32), jnp.zeros((tq, pair_w), F32)) for _ in range(heads_per_step))
    per_q = tq // tk
    for d in range(per_q):
        carry = tiles(pl.multiple_of(q_start + (per_q - 1 - d) * tk, tk), carry, True)
    n_before = qi * per_q
    carry = lax.fori_loop(0, n_before, lambda jj, c: tiles(pl.multiple_of((n_before - 1 - jj) * tk, tk), c, False),
                          carry)
    for p in range(heads_per_step // 2):
        o_ref[:, cols[2 * p]] = jnp.where(lane < dh, carry[2 * p][1], carry[2 * p + 1][1]).astype(o_ref.dtype)


def _strict_lower_ones(n, dtype):
    s = lax.broadcasted_iota(jnp.int32, (n, n), 0)
    j = lax.broadcasted_iota(jnp.int32, (n, n), 1)
    return (s > j).astype(dtype)


def _sb_prompt(qs, kb, vb, sb_bias, o_init, *, batch, seq, heads, dh, tq, tk):
    nt, wa = qs.shape
    nq = seq // tq
    tri = _strict_lower_ones(tk, BF16)
    hps = SB_HEADS_PER_STEP
    grid_spec = pltpu.PrefetchScalarGridSpec(
        num_scalar_prefetch=1, grid=(batch, heads // hps, nq),
        in_specs=[
            pl.BlockSpec((tq, hps * dh), lambda b, h, i, *_: (b * nq + i, h)),
            pl.BlockSpec((seq, hps * dh), lambda b, h, i, *_: (b, h)),
            pl.BlockSpec((seq, hps * dh), lambda b, h, i, *_: (b, h)),
            pl.BlockSpec(tri.shape, lambda b, h, i, *_: (0, 0)),
            pl.BlockSpec(memory_space=pl.ANY),
        ],
        out_specs=pl.BlockSpec((tq, hps * dh), lambda b, h, i, *_: (b * nq + i, h)),
    )
    return pl.pallas_call(
        functools.partial(_sb_prompt_kernel, tq=tq, tk=tk, dh=dh, heads_per_step=hps), grid_spec=grid_spec,
        out_shape=jax.ShapeDtypeStruct((nt, wa), BF16), input_output_aliases={5: 0},
        compiler_params=_params(("parallel", "parallel", "arbitrary")), name="sb_prompt")(
            sb_bias, qs, kb, vb, tri, o_init)


def _sb_sample_kernel(pt_ref, bias_ref, qsel_ref, kn_ref, vn_ref, tri_ref, *refs,
                      pages_per_step, page, heads, past):
    k_refs = refs[:pages_per_step]
    v_refs = refs[pages_per_step:2 * pages_per_step]
    o_ref, run_ref, acc_ref = refs[2 * pages_per_step:]
    del pt_ref
    b = pl.program_id(0)
    g = pl.program_id(1)
    hrows = qsel_ref.shape[2]
    bias = bias_ref[...]
    tri = tri_ref[...]
    nt_dims = (((1,), (1,)), ((), ()))

    @pl.when(g == 0)
    def _():
        q_rows = qsel_ref[0, 0].astype(F32)
        for h in range(1, heads):
            q_rows = q_rows + qsel_ref[0, h].astype(F32)
        z = jnp.sum(q_rows * kn_ref[0], axis=-1, keepdims=True) + bias
        visible = jnp.full(z.shape, past, jnp.int32) < jnp.full(z.shape, past, jnp.int32)
        lr = jnp.where(visible, _neg_softplus(z), 0.0)
        w = jnp.where(visible, jnp.exp(z + lr), 0.0)
        run_ref[...] = lr
        for h in range(heads):
            acc_ref[h] = w * vn_ref[0, h:h + 1, :]

    zs = []
    for s in range(pages_per_step):
        z = bias
        for h in range(heads):
            z = z + jnp.dot(qsel_ref[0, h], k_refs[s][0, 0, h].astype(BF16), preferred_element_type=F32)
        zs.append(z)
    z = jnp.concatenate(zs, axis=0)
    lr = _neg_softplus(z)
    hi, lo = _split_bf16(lr)
    aft = jnp.dot(hi, tri, preferred_element_type=F32) + jnp.dot(lo, tri, preferred_element_type=F32)
    total = aft[:, 0:1] + lr[:, 0:1]
    run = run_ref[...]
    runs = []
    for s in range(pages_per_step):
        runs.append(run)
        run = run + total[s * hrows:(s + 1) * hrows]
    run_ref[...] = run
    w = jnp.exp(z + lr + aft + jnp.concatenate(runs, axis=0)).astype(BF16)
    for h in range(heads):
        acc = acc_ref[h]
        for s in range(pages_per_step):
            acc = acc + lax.dot_general(w[s * hrows:(s + 1) * hrows], v_refs[s][0, 0, h].astype(BF16), nt_dims,
                                        preferred_element_type=F32)
        acc_ref[h] = acc

    @pl.when(g == pl.num_programs(1) - 1)
    def _():
        for h in range(heads):
            o_ref[pl.ds(b, 1), h:h + 1, :] = acc_ref[h, h:h + 1, :][None]


def _sb_sample(q, k_new, v_new, cache_k, cache_v, page_table, sb_bias, *, layer, heads, dh, pages_per_step):
    s_rows, wa = q.shape
    n_pages = page_table.shape[1]
    page = cache_k.shape[2]
    n_phys = cache_k.shape[1]
    ck = jnp.transpose(cache_k, (0, 1, 3, 4, 2))
    cv = jnp.transpose(cache_v, (0, 1, 3, 4, 2))
    hrows = 2 * heads
    own_row = (lax.broadcasted_iota(jnp.int32, (heads, hrows, 1), 1)
               == lax.broadcasted_iota(jnp.int32, (heads, hrows, 1), 0))
    qsel = jnp.where(own_row[None], q.reshape(s_rows, heads, 1, dh), 0.0).astype(BF16)
    k_own = jnp.pad(k_new.reshape(s_rows, heads, dh), ((0, 0), (0, hrows - heads), (0, 0)))
    v_own = v_new.reshape(s_rows, heads, dh)
    bias = jnp.concatenate([sb_bias, jnp.zeros((hrows - heads,), F32)])[:, None]
    tri = _strict_lower_ones(page, BF16)
    steps = n_pages // pages_per_step

    def page_map(s):
        def index_map(b, g, pt):
            return (layer, pt[b, n_pages - 1 - (g * pages_per_step + s)], 0, 0, 0)
        return index_map

    page_specs = [pl.BlockSpec((1, 1, heads, dh, page), page_map(s)) for s in range(pages_per_step)]
    grid_spec = pltpu.PrefetchScalarGridSpec(
        num_scalar_prefetch=1, grid=(s_rows, steps),
        in_specs=[
            pl.BlockSpec((hrows, 1), lambda b, g, pt: (0, 0)),
            pl.BlockSpec((1, heads, hrows, dh), lambda b, g, pt: (b, 0, 0, 0)),
            pl.BlockSpec((1, hrows, dh), lambda b, g, pt: (b, 0, 0)),
            pl.BlockSpec((1, heads, dh), lambda b, g, pt: (b, 0, 0)),
            pl.BlockSpec((page, page), lambda b, g, pt: (0, 0)),
        ] + page_specs + page_specs,
        out_specs=pl.BlockSpec((s_rows, heads, dh), lambda b, g, pt: (0, 0, 0)),
        scratch_shapes=[pltpu.VMEM((hrows, 1), F32), pltpu.VMEM((heads, hrows, dh), F32)],
    )
    out = pl.pallas_call(
        functools.partial(_sb_sample_kernel, pages_per_step=pages_per_step, page=page, heads=heads,
                          past=n_pages * page),
        grid_spec=grid_spec, out_shape=jax.ShapeDtypeStruct((s_rows, heads, dh), F32),
        compiler_params=_params(("arbitrary", "arbitrary")), name="sb_sample")(
            page_table, bias, qsel, k_own, v_own, tri, *([ck] * pages_per_step), *([cv] * pages_per_step))
    return out.reshape(s_rows, wa)


def _lru_gates(xc, wa_ref, ba_ref, wx_ref, bx_ref, lam_ref):
    xb = xc.astype(BF16)
    r = _sigmoid(jnp.dot(xb, wa_ref[...], preferred_element_type=F32) + ba_ref[...])
    i = _sigmoid(jnp.dot(xb, wx_ref[...], preferred_element_type=F32) + bx_ref[...])
    log_a = LRU_C * r * _neg_softplus(-lam_ref[...])
    a = jnp.exp(log_a)
    u = jnp.sqrt(-jnp.tanh(log_a) * (a * a + 1.0)) * (i * xc)
    return a, u


def _lru_prompt_kernel(lx_ref, glg_ref, cw_ref, cb_ref, wa_ref, ba_ref, wx_ref, bx_ref, lam_ref, ob_init,
                       ob_ref, h_ref, buf_ref, xpad, a_s, u_s, *, seq, chunk, taps):
    del ob_init
    width = lx_ref.shape[1]
    xpad[0:SUBLANES, :] = jnp.zeros((SUBLANES, width), F32)
    xpad[SUBLANES:, :] = lx_ref[...]
    rowid = lax.broadcasted_iota(jnp.int32, (SUBLANES, width), 0)
    h = jnp.zeros((1, width), F32)
    for c in range(seq // chunk):
        base = SUBLANES + c * chunk
        xc = cb_ref[...] + cw_ref[taps - 1:taps, :] * xpad[base:base + chunk, :]
        for d in range(1, taps):
            xc = xc + cw_ref[taps - 1 - d:taps - d, :] * xpad[base - d:base - d + chunk, :]
        a, u = _lru_gates(xc, wa_ref, ba_ref, wx_ref, bx_ref, lam_ref)
        a_s[...] = a
        u_s[...] = u

        def group(gi, h):
            r0 = pl.multiple_of(gi * SUBLANES, SUBLANES)
            ga = a_s[pl.ds(r0, SUBLANES), :]
            gu = u_s[pl.ds(r0, SUBLANES), :]
            for d in (1, 2, 4):
                keep = rowid >= d
                gu = jnp.where(keep, ga * pltpu.roll(gu, d, 0) + gu, gu)
                ga = jnp.where(keep, ga * pltpu.roll(ga, d, 0), ga)
            hs = ga * h + gu
            u_s[pl.ds(r0, SUBLANES), :] = hs
            return hs[SUBLANES - 1:SUBLANES, :]

        h = lax.fori_loop(0, chunk // SUBLANES, group, h)
        rows = slice(c * chunk, (c + 1) * chunk)
        ob_ref[rows, :] = (u_s[...] * glg_ref[rows, :].astype(F32)).astype(ob_ref.dtype)
    h_ref[0] = h
    buf_ref[0] = lx_ref[seq - (taps - 1):seq, :]


def _lru_prompt(lx, glg, cw, cb, wa_bd, ba, wx_bd, bx, lam, ob_init, *, batch, seq, chunk):
    nt, width = lx.shape
    taps = cw.shape[0]
    consts = [cw, cb, wa_bd, ba, wx_bd, bx, lam]
    return pl.pallas_call(
        functools.partial(_lru_prompt_kernel, seq=seq, chunk=chunk, taps=taps),
        grid=(batch,),
        in_specs=[pl.BlockSpec((seq, width), lambda b: (b, 0)),
                  pl.BlockSpec((seq, width), lambda b: (b, 0))] + [_full_spec(a) for a in consts]
        + [pl.BlockSpec(memory_space=pl.ANY)],
        input_output_aliases={2 + len(consts): 0},
        out_specs=[pl.BlockSpec((seq, width), lambda b: (b, 0)),
                   pl.BlockSpec((1, 1, width), lambda b: (b, 0, 0)),
                   pl.BlockSpec((1, taps - 1, width), lambda b: (b, 0, 0))],
        out_shape=[jax.ShapeDtypeStruct((nt, width), BF16),
                   jax.ShapeDtypeStruct((batch, 1, width), F32),
                   jax.ShapeDtypeStruct((batch, taps - 1, width), F32)],
        scratch_shapes=[pltpu.VMEM((seq + SUBLANES, width), F32),
                        pltpu.VMEM((chunk, width), F32), pltpu.VMEM((chunk, width), F32)],
        compiler_params=_params(("parallel",)), name="lru_prompt")(lx, glg, *consts, ob_init)


def _conf_prompt_kernel(glu_ref, cw_ref, cb_ref, g_ref, b_ref, oc_init, oc_ref, buf_ref, xpad, win, cc_s,
                        *, seq, chunk, taps, pad):
    del oc_init
    width = glu_ref.shape[1]
    xpad[0:pad, :] = jnp.zeros((pad, width), F32)
    xpad[pad:, :] = glu_ref[...]
    lead = pad - (taps - 1)

    def one_chunk(c, carry):
        t0 = pl.multiple_of(c * chunk, chunk)
        win[...] = xpad[pl.ds(t0, chunk + pad), :]
        for l0 in range(0, width, LANES):
            lanes = slice(l0, l0 + LANES)
            acc = cb_ref[:, lanes] + cw_ref[0:1, lanes] * win[lead:lead + chunk, lanes]
            for j in range(1, taps):
                acc = acc + cw_ref[j:j + 1, lanes] * win[lead + j:lead + j + chunk, lanes]
            cc_s[:, lanes] = acc
        y = _layer_norm(cc_s[...], g_ref[...], b_ref[...])
        oc_ref[pl.ds(t0, chunk), :] = (y * _sigmoid(y)).astype(oc_ref.dtype)
        return carry

    lax.fori_loop(0, seq // chunk, one_chunk, 0)
    buf_ref[0] = glu_ref[seq - (taps - 1):seq, :]


def _conf_prompt(glu, cw, cb, g, b, oc_init, *, batch, seq, chunk):
    nt, width = glu.shape
    taps = cw.shape[0]
    pad = -(-(taps - 1) // SUBLANES) * SUBLANES
    consts = [cw, cb, g, b]
    return pl.pallas_call(
        functools.partial(_conf_prompt_kernel, seq=seq, chunk=chunk, taps=taps, pad=pad),
        grid=(batch,),
        in_specs=[pl.BlockSpec((seq, width), lambda b: (b, 0))] + [_full_spec(a) for a in consts]
        + [pl.BlockSpec(memory_space=pl.ANY)],
        input_output_aliases={1 + len(consts): 0},
        out_specs=[pl.BlockSpec((seq, width), lambda b: (b, 0)),
                   pl.BlockSpec((1, taps - 1, width), lambda b: (b, 0, 0))],
        out_shape=[jax.ShapeDtypeStruct((nt, width), BF16),
                   jax.ShapeDtypeStruct((batch, taps - 1, width), F32)],
        scratch_shapes=[pltpu.VMEM((seq + pad, width), F32), pltpu.VMEM((chunk + pad, width), F32),
                        pltpu.VMEM((chunk, width), F32)],
        compiler_params=_params(("parallel",)), name="conf_prompt")(glu, *consts, oc_init)


def _sample_mix_kernel(lx_ref, glg_ref, glu_ref, oas_ref, h0_ref, lbuf_ref, cbuf_ref,
                       lcw_ref, lcb_ref, wa_ref, ba_ref, wx_ref, bx_ref, lam_ref,
                       ccw_ref, ccb_ref, cg_ref, cbeta_ref, oa_in, ob_in, oc_in,
                       oa_ref, ob_ref, oc_ref, h_ref, lbuf_out, cbuf_out, *, ltaps, ctaps):
    del oa_in, ob_in, oc_in
    oa_ref[...] = oas_ref[...].astype(oa_ref.dtype)
    lx = lx_ref[...]
    xc = lcb_ref[...] + lcw_ref[ltaps - 1:ltaps, :] * lx
    for j in range(ltaps - 1):
        xc = xc + lcw_ref[j:j + 1, :] * lbuf_ref[j]
    a, u = _lru_gates(xc, wa_ref, ba_ref, wx_ref, bx_ref, lam_ref)
    h = a * h0_ref[...] + u
    h_ref[...] = h
    ob_ref[...] = (h * glg_ref[...].astype(F32)).astype(ob_ref.dtype)
    for j in range(ltaps - 2):
        lbuf_out[j] = lbuf_ref[j + 1]
    lbuf_out[ltaps - 2] = lx

    glu = glu_ref[...]
    cc = ccb_ref[...] + ccw_ref[ctaps - 1:ctaps, :] * glu
    for j in range(ctaps - 1):
        cc = cc + ccw_ref[j:j + 1, :] * cbuf_ref[j]
    y = _layer_norm(cc, cg_ref[...], cbeta_ref[...])
    oc_ref[...] = (y * _sigmoid(y)).astype(oc_ref.dtype)
    for j in range(ctaps - 2):
        cbuf_out[j] = cbuf_ref[j + 1]
    cbuf_out[ctaps - 2] = glu


def _sample_mix(lx, glg, glu, oa_s, h0, lbuf, cbuf, lru_consts, conf_consts, oa_all, ob_all, oc_all, *, row0, rows):
    width = lx.shape[1]
    blk = row0 // rows
    ltaps = lru_consts[0].shape[0]
    ctaps = conf_consts[0].shape[0]
    lbuf_t = jnp.transpose(lbuf, (1, 0, 2))
    cbuf_t = jnp.transpose(cbuf, (1, 0, 2))
    consts = list(lru_consts) + list(conf_consts)
    tail = pl.BlockSpec((rows, width), lambda i: (blk, 0))
    any_spec = pl.BlockSpec(memory_space=pl.ANY)
    ins = [lx, glg, glu, oa_s, h0, lbuf_t, cbuf_t] + consts + [oa_all, ob_all, oc_all]
    in_specs = ([tail, tail, tail, _full_spec(oa_s), _full_spec(h0), _full_spec(lbuf_t), _full_spec(cbuf_t)]
                + [_full_spec(a) for a in consts] + [any_spec, any_spec, any_spec])
    n_in = len(ins)
    oa, ob, oc, h, lb, cb = pl.pallas_call(
        functools.partial(_sample_mix_kernel, ltaps=ltaps, ctaps=ctaps),
        grid=(1,), in_specs=in_specs,
        out_specs=[tail, tail, tail, _full_spec(h0), _full_spec(lbuf_t), _full_spec(cbuf_t)],
        out_shape=[jax.ShapeDtypeStruct(a.shape, a.dtype) for a in (oa_all, ob_all, oc_all)]
        + [jax.ShapeDtypeStruct(h0.shape, F32), jax.ShapeDtypeStruct(lbuf_t.shape, F32),
           jax.ShapeDtypeStruct(cbuf_t.shape, F32)],
        input_output_aliases={n_in - 3: 0, n_in - 2: 1, n_in - 1: 2},
        compiler_params=_params(("arbitrary",)), name="sample_mix")(*ins)
    return oa, ob, oc, h, jnp.transpose(lb, (1, 0, 2)), jnp.transpose(cb, (1, 0, 2))


def _merge_body(rows, consts, outs, *, alpha, n_experts, d_model):
    x_ref, oa_ref, ob_ref, oc_ref = rows
    wgate_ref, bgate_ref, wbr_ref, wout_ref, g_ref, b_ref, rw_ref, rb_ref = consts
    x1_ref, ridx_ref, rgw_ref = outs
    x = x_ref[...]
    xb = x.astype(BF16)
    m = None
    for i, o_ref in enumerate((oa_ref, ob_ref, oc_ref)):
        cols = slice(i * d_model, (i + 1) * d_model)
        gate = _sigmoid(jnp.dot(xb, wgate_ref[:, cols], preferred_element_type=F32) + bgate_ref[:, cols])
        p = gate * jnp.dot(o_ref[...], wbr_ref[i], preferred_element_type=F32)
        m = p if m is None else m + p
    y = alpha * x + jnp.dot(m.astype(BF16), wout_ref[...], preferred_element_type=F32)
    x1 = _layer_norm(y, g_ref[...], b_ref[...])
    x1_ref[...] = x1
    logits = jnp.dot(x1.astype(BF16), rw_ref[...], preferred_element_type=F32) + rb_ref[...]
    lane = lax.broadcasted_iota(jnp.int32, logits.shape, 1)
    lanef = lane.astype(F32)
    logits = jnp.where(lane < n_experts, logits, NEG_INF)
    idx_out = jnp.zeros(logits.shape, F32)
    val_out = jnp.zeros(logits.shape, F32)
    top = None
    for k in range(TOP_K):
        mx = jnp.max(logits, axis=-1, keepdims=True)
        sel = jnp.min(jnp.where(logits == mx, lanef, float(LANES)), axis=-1, keepdims=True)
        top = mx if top is None else top
        idx_out = jnp.where(lane == k, sel, idx_out)
        val_out = jnp.where(lane == k, jnp.exp(mx - top), val_out)
        logits = jnp.where(lanef == sel, NEG_INF, logits)
    ridx_ref[...] = idx_out.astype(jnp.int32)
    rgw_ref[...] = val_out / jnp.sum(val_out, axis=-1, keepdims=True)


def _for_each_row(tt, fn):
    def body(g, c):
        for u in range(ROW_UNROLL):
            fn(g * ROW_UNROLL + u)
        return c

    lax.fori_loop(0, tt // ROW_UNROLL, body, 0)


def _dispatch_kernel(dest_hbm, x_ref, xs_in, xs_hbm, dest_s, sem_i, sem_r, *, tt):
    del xs_in
    i = pl.program_id(0)
    cp = pltpu.make_async_copy(dest_hbm.at[i], dest_s, sem_i)
    cp.start()
    cp.wait()

    def copies(r):
        src = x_ref.at[pl.ds(r, 1), :]
        return [pltpu.make_async_copy(src, xs_hbm.at[pl.ds(dest_s[0, r * TOP_K + k], 1), :], sem_r)
                for k in range(TOP_K)]

    _for_each_row(tt, lambda r: [c.start() for c in copies(r)])
    _for_each_row(tt, lambda r: [c.wait() for c in copies(r)])


def _dispatch(x1, dest2, xs_zero, *, tt):
    nt, d = x1.shape
    any_spec = pl.BlockSpec(memory_space=pl.ANY)
    return pl.pallas_call(
        functools.partial(_dispatch_kernel, tt=tt), grid=(nt // tt,),
        in_specs=[any_spec, pl.BlockSpec((tt, d), lambda i: (i, 0)), any_spec], out_specs=any_spec,
        out_shape=jax.ShapeDtypeStruct(xs_zero.shape, xs_zero.dtype),
        scratch_shapes=[pltpu.SMEM((1, tt * TOP_K), jnp.int32), pltpu.SemaphoreType.DMA, pltpu.SemaphoreType.DMA],
        input_output_aliases={2: 0},
        compiler_params=_params(("arbitrary",)), name="moe_dispatch")(dest2, x1, xs_zero)


def _expert_kernel(be_ref, nu_ref, first_ref, x_ref, wg_ref, bg_ref, wu_ref, bu_ref, wd_ref, bd_ref, y_ref,
                   wg_s, wu_s, wd_s):
    i = pl.program_id(0)

    @pl.when(first_ref[i] == 1)
    def _():
        wg_s[...] = wg_ref[0, 0].astype(BF16)
        wu_s[...] = wu_ref[0, 0].astype(BF16)
        wd_s[...] = wd_ref[0, 0].astype(BF16)

    @pl.when(i < nu_ref[0])
    def _():
        xb = x_ref[...].astype(BF16)
        g = jnp.minimum(jnp.dot(xb, wg_s[...], preferred_element_type=F32) + bg_ref[0], SWIGLU_LIMIT)
        u = jnp.clip(jnp.dot(xb, wu_s[...], preferred_element_type=F32) + bu_ref[0], -SWIGLU_LIMIT, SWIGLU_LIMIT)
        h = (u + 1.0) * (g * _sigmoid(SWIGLU_ALPHA * g))
        y_ref[...] = jnp.dot(h.astype(BF16), wd_s[...], preferred_element_type=F32) + bd_ref[0]

    @pl.when(i >= nu_ref[0])
    def _():
        y_ref[...] = jnp.zeros(y_ref.shape, y_ref.dtype)


def _experts(xs, block_e, n_used, first, wg, bg, wu, bu, wd, bd, *, layer):
    rows, d = xs.shape
    dff = wg.shape[3]
    nb = rows // EXPERT_ROWS
    wspec = lambda shape: pl.BlockSpec((1, 1) + shape, lambda i, be, nu, fi: (layer, be[i], 0, 0))
    bspec = lambda n: pl.BlockSpec((1, 1, n), lambda i, be, nu, fi: (be[i], 0, 0))
    grid_spec = pltpu.PrefetchScalarGridSpec(
        num_scalar_prefetch=3, grid=(nb,),
        in_specs=[pl.BlockSpec((EXPERT_ROWS, d), lambda i, be, nu, fi: (i, 0)),
                  wspec((d, dff)), bspec(dff), wspec((d, dff)), bspec(dff), wspec((dff, d)), bspec(d)],
        out_specs=pl.BlockSpec((EXPERT_ROWS, d), lambda i, be, nu, fi: (i, 0)),
        scratch_shapes=[pltpu.VMEM((d, dff), BF16), pltpu.VMEM((d, dff), BF16), pltpu.VMEM((dff, d), BF16)])
    return pl.pallas_call(
        _expert_kernel, grid_spec=grid_spec, out_shape=jax.ShapeDtypeStruct((rows, d), F32),
        compiler_params=_params(("arbitrary",)), name="moe_experts")(
            block_e, n_used, first, xs, wg, bg[:, None, :], wu, bu[:, None, :], wd, bd[:, None, :])


def _combine_kernel(dest_hbm, ys_hbm, x1_ref, gw_ref, g_ref, b_ref, x2_ref, dest_s, ybuf, sem_i, sem_r, *, tt, alpha):
    i = pl.program_id(0)
    slot = i % 2

    def gathers(r, s):
        return [pltpu.make_async_copy(ys_hbm.at[pl.ds(dest_s[s, 0, r * TOP_K + k], 1), :],
                                      ybuf.at[s, k, pl.ds(r, 1), :], sem_r.at[s]) for k in range(TOP_K)]

    def fetch(tile, s):
        cp = pltpu.make_async_copy(dest_hbm.at[tile], dest_s.at[s], sem_i)
        cp.start()
        cp.wait()
        _for_each_row(tt, lambda r: [c.start() for c in gathers(r, s)])

    @pl.when(i == 0)
    def _():
        fetch(0, 0)

    @pl.when(i + 1 < pl.num_programs(0))
    def _():
        fetch(i + 1, 1 - slot)

    _for_each_row(tt, lambda r: [c.wait() for c in gathers(r, slot)])
    gw = gw_ref[...]
    moe = gw[:, 0:1] * ybuf[slot, 0]
    for k in range(1, TOP_K):
        moe = moe + gw[:, k:k + 1] * ybuf[slot, k]
    x2_ref[...] = _layer_norm(alpha * x1_ref[...] + moe, g_ref[...], b_ref[...])


def _combine(ys, dest2, x1, rgw, g, b, *, tt, alpha):
    nt, d = x1.shape
    return pl.pallas_call(
        functools.partial(_combine_kernel, tt=tt, alpha=alpha), grid=(nt // tt,),
        in_specs=[pl.BlockSpec(memory_space=pl.ANY), pl.BlockSpec(memory_space=pl.ANY),
                  pl.BlockSpec((tt, d), lambda i: (i, 0)), pl.BlockSpec((tt, LANES), lambda i: (i, 0)),
                  _full_spec(g), _full_spec(b)],
        out_specs=pl.BlockSpec((tt, d), lambda i: (i, 0)),
        out_shape=jax.ShapeDtypeStruct((nt, d), F32),
        scratch_shapes=[pltpu.SMEM((2, 1, tt * TOP_K), jnp.int32), pltpu.VMEM((2, TOP_K, tt, d), F32),
                        pltpu.SemaphoreType.DMA, pltpu.SemaphoreType.DMA((2,))],
        compiler_params=_params(("arbitrary",)), name="moe_combine")(dest2, ys, x1, rgw, g, b)


def _route_rank_kernel(ridx_ref, tri_ref, rank_ref, cnt_ref, carry):
    @pl.when(pl.program_id(0) == 0)
    def _():
        carry[...] = jnp.zeros(carry.shape, F32)

    idx = ridx_ref[...]
    lane = lax.broadcasted_iota(jnp.int32, idx.shape, 1)
    onehot = jnp.zeros(idx.shape, F32)
    for k in range(TOP_K):
        onehot = onehot + (idx[:, k:k + 1] == lane).astype(F32)
    rank_ref[...] = jnp.dot(tri_ref[...], onehot.astype(BF16), preferred_element_type=F32) + carry[...]
    carry[...] = carry[...] + jnp.sum(onehot, axis=0, keepdims=True)
    cnt_ref[...] = carry[...]


def _route(ridx, n_experts, tt):
    nt = ridx.shape[0]
    n_assign = nt * TOP_K
    tri = _strict_lower_ones(tt, BF16)
    rank, cnt = pl.pallas_call(
        _route_rank_kernel, grid=(nt // tt,),
        in_specs=[pl.BlockSpec((tt, LANES), lambda i: (i, 0)), _full_spec(tri)],
        out_specs=[pl.BlockSpec((tt, LANES), lambda i: (i, 0)), pl.BlockSpec((1, LANES), lambda i: (0, 0))],
        out_shape=[jax.ShapeDtypeStruct((nt, LANES), F32), jax.ShapeDtypeStruct((1, LANES), F32)],
        scratch_shapes=[pltpu.VMEM((1, LANES), F32)],
        compiler_params=_params(("arbitrary",)), name="route_rank")(ridx, tri)
    counts = cnt[0, :n_experts].astype(jnp.int32)
    padded = (counts + EXPERT_ROWS - 1) // EXPERT_ROWS * EXPERT_ROWS
    pend = jnp.cumsum(padded)
    pstart = pend - padded
    slot = rank[:, :n_experts].astype(jnp.int32) + pstart[None, :]
    dest = jnp.take_along_axis(slot, ridx[:, :TOP_K], axis=1)
    n_blocks = -(-n_assign // EXPERT_ROWS) + n_experts
    block_row0 = jnp.arange(n_blocks, dtype=jnp.int32) * EXPERT_ROWS
    block_e = jnp.minimum(jnp.sum((pend[None, :] <= block_row0[:, None]).astype(jnp.int32), axis=1), n_experts - 1)
    first = jnp.concatenate([jnp.ones((1,), jnp.int32), (block_e[1:] != block_e[:-1]).astype(jnp.int32)])
    n_used = (pend[-1:] // EXPERT_ROWS).astype(jnp.int32)
    return dest.reshape(nt // tt, 1, tt * TOP_K), block_e, n_used, first, n_blocks


def _block_diag(w):
    nb, bi, bo = w.shape
    eye = jnp.eye(nb, dtype=w.dtype)
    return (w[:, :, None, :] * eye[:, None, :, None]).reshape(nb * bi, nb * bo)


def _largest_tile(n, limit, align):
    best = align
    for t in range(align, limit + 1, align):
        if n % t == 0:
            best = t
    return best


def kernel(x_prompt, x_sample, cache_k, cache_v, state_lru_h, state_lru_conv, state_conf_conv, page_table, w_in, b_merge, sb_bias, lru_conv_w, lru_conv_b, lru_wa, lru_ba, lru_wx, lru_bx, lru_lambda, conf_conv_w, conf_conv_b, conf_ln_g, conf_ln_b, w_branch, w_out, ln1_g, ln1_b, router_w, router_b, moe_w_gate, moe_b_gate, moe_w_up, moe_b_up, moe_w_down, moe_b_down, ln2_g, ln2_b):
    batch, seq, d_model = x_prompt.shape
    s_rows = x_sample.shape[0]
    depth = w_in.shape[0]
    heads, dh = cache_k.shape[3], cache_k.shape[4]
    wa = heads * dh
    wb = lru_conv_w.shape[2]
    wc = conf_conv_w.shape[2]
    n_experts = router_w.shape[2]
    n_prompt = batch * seq
    nt = n_prompt + s_rows
    alpha = float((2 * depth) ** 0.25)
    tm = _largest_tile(nt, 640, 16)
    tt = _largest_tile(nt, 320, 8)
    row = lambda v: v[None, :]

    x = jnp.concatenate([x_prompt.reshape(n_prompt, d_model), x_sample.reshape(s_rows, d_model)], axis=0)
    outs = {k: [] for k in ("kp", "vp", "ks", "vs", "hp", "hs", "lp", "ls", "cp", "cs")}
    for l in range(depth):
        w_l = w_in[l].astype(BF16)
        c0, c1, c2 = 3 * wa, 3 * wa + 2 * wb + 2 * wc, w_in.shape[2]
        qs, k, v, kb, vb = _tokenwise_call(
            functools.partial(_proj_qkv_body, wa=wa, scale=dh ** -0.5), nt, tm, [x], [w_l[:, :c0]],
            [(wa, BF16), (wa, F32), (wa, F32), (wa, BF16), (wa, BF16)], "proj_qkv")
        lx, glg, glu = _tokenwise_call(
            functools.partial(_proj_mix_body, wb=wb, wc=wc), nt, tm, [x], [w_l[:, c0:c1]],
            [(wb, F32), (wb, F32), (wc, F32)], "proj_mix")

        o_a = _sb_prompt(qs, kb, vb, sb_bias[l], jnp.zeros((nt, wa), BF16), batch=batch, seq=seq, heads=heads,
                         dh=dh, tq=512, tk=512)
        q_s = qs[n_prompt:].astype(F32)
        oa_s = _sb_sample(q_s, k[n_prompt:], v[n_prompt:], cache_k, cache_v, page_table, sb_bias[l],
                          layer=l, heads=heads, dh=dh, pages_per_step=8)

        lru_consts = [lru_conv_w[l], row(lru_conv_b[l]), _block_diag(lru_wa[l]).astype(BF16), row(lru_ba[l]),
                      _block_diag(lru_wx[l]).astype(BF16), row(lru_bx[l]), row(lru_lambda[l])]
        conf_consts = [conf_conv_w[l], row(conf_conv_b[l]), row(conf_ln_g[l]), row(conf_ln_b[l])]
        o_b, h_p, lbuf_p = _lru_prompt(lx, glg, *lru_consts, jnp.zeros((nt, wb), BF16), batch=batch, seq=seq,
                                       chunk=256)
        o_c, cbuf_p = _conf_prompt(glu, *conf_consts, jnp.zeros((nt, wc), BF16), batch=batch, seq=seq, chunk=128)
        o_a, o_b, o_c, h_s, lbuf_s, cbuf_s = _sample_mix(
            lx, glg, glu, oa_s, state_lru_h[l], state_lru_conv[l], state_conf_conv[l], lru_consts, conf_consts,
            o_a, o_b, o_c, row0=n_prompt, rows=s_rows)

        rw = jnp.zeros((d_model, LANES), BF16).at[:, :n_experts].set(router_w[l].astype(BF16))
        rb = jnp.zeros((1, LANES), F32).at[0, :n_experts].set(router_b[l])
        x1, ridx, rgw = _tokenwise_call(
            functools.partial(_merge_body, alpha=alpha, n_experts=n_experts, d_model=d_model), nt, tm,
            [x, o_a, o_b, o_c],
            [w_l[:, c1:c2], b_merge[l].reshape(1, c2 - c1), w_branch[l].astype(BF16), w_out[l].astype(BF16),
             row(ln1_g[l]), row(ln1_b[l]), rw, rb],
            [(d_model, F32), (LANES, jnp.int32), (LANES, F32)], "merge")

        dest2, block_e, n_used, first, n_blocks = _route(ridx, n_experts, tt)
        xs = _dispatch(x1, dest2, jnp.zeros((n_blocks * EXPERT_ROWS, d_model), F32), tt=tt)
        ys = _experts(xs, block_e, n_used, first, moe_w_gate, moe_b_gate[l], moe_w_up, moe_b_up[l],
                      moe_w_down, moe_b_down[l], layer=l)
        x = _combine(ys, dest2, x1, rgw, row(ln2_g[l]), row(ln2_b[l]), tt=tt, alpha=alpha)

        outs["kp"].append(k[:n_prompt].reshape(batch, seq, heads, dh))
        outs["vp"].append(v[:n_prompt].reshape(batch, seq, heads, dh))
        outs["ks"].append(k[n_prompt:].reshape(s_rows, 1, heads, dh))
        outs["vs"].append(v[n_prompt:].reshape(s_rows, 1, heads, dh))
        outs["hp"].append(h_p.reshape(batch, wb))
        outs["hs"].append(h_s)
        outs["lp"].append(lbuf_p)
        outs["ls"].append(lbuf_s)
        outs["cp"].append(cbuf_p)
        outs["cs"].append(cbuf_s)

    st = {k_: jnp.stack(v_) for k_, v_ in outs.items()}
    return (x[:n_prompt].reshape(batch, seq, d_model), x[n_prompt:].reshape(s_rows, 1, d_model),
            st["kp"], st["vp"], st["ks"], st["vs"], st["hp"], st["hs"], st["lp"], st["ls"], st["cp"], st["cs"])
```

```python
import functools

import jax
import jax.numpy as jnp
from jax import lax
from jax.experimental import pallas as pl
from jax.experimental.pallas import tpu as pltpu

F32 = jnp.float32
BF16 = jnp.bfloat16

LN_EPS = 1e-5
LRU_C = 8.0
SWIGLU_LIMIT = 7.0
SWIGLU_ALPHA = 1.702
TOP_K = 4
LANES = 128
SUBLANES = 8
VMEM_LIMIT = 56 * 1024 * 1024
EXPERT_ROWS = 512
SB_HEADS_PER_STEP = 4
ROW_UNROLL = 8
NEG_INF = float("-inf")
LOG2E = 1.4426950408889634


def _sigmoid(x):
    return 1.0 / (1.0 + jnp.exp(-x))


def _neg_softplus(z):
    return -(jnp.maximum(z, 0.0) + jnp.log(1.0 + jnp.exp(-jnp.abs(z))))


def _neg_abs(x):
    bits = lax.bitcast_convert_type(x, jnp.uint32) | jnp.uint32(0x80000000)
    return lax.bitcast_convert_type(bits, F32)


def _layer_norm(y, g, b):
    mu = jnp.mean(y, axis=-1, keepdims=True)
    d = y - mu
    var = jnp.mean(d * d, axis=-1, keepdims=True)
    return d * lax.rsqrt(var + LN_EPS) * g + b


def _split_bf16(x):
    hi = x.astype(BF16)
    lo = (x - hi.astype(F32)).astype(BF16)
    return hi, lo


def _params(sem=None):
    return pltpu.CompilerParams(dimension_semantics=sem, vmem_limit_bytes=VMEM_LIMIT)


def _full_spec(a):
    nd = a.ndim
    return pl.BlockSpec(a.shape, lambda *_: (0,) * nd)


def _tokenwise_call(body, n_rows, tm, row_ins, const_ins, outs, name):
    in_specs = [pl.BlockSpec((tm, a.shape[1]), lambda i: (i, 0)) for a in row_ins]
    in_specs += [_full_spec(a) for a in const_ins]
    out_shape = [jax.ShapeDtypeStruct((n_rows, c), dt) for c, dt in outs]
    out_specs = [pl.BlockSpec((tm, c), lambda i: (i, 0)) for c, _ in outs]
    nr, nc = len(row_ins), len(const_ins)

    def kernel(*refs):
        body(refs[:nr], refs[nr:nr + nc], refs[nr + nc:])

    return pl.pallas_call(
        kernel, grid=(n_rows // tm,), in_specs=in_specs, out_specs=out_specs, out_shape=out_shape,
        compiler_params=_params(("parallel",)), name=name)(*row_ins, *const_ins)


def _proj_qkv_body(rows, consts, outs, *, wa, scale):
    (x_ref,), (w_ref,) = rows, consts
    q_ref, k_ref, v_ref, kb_ref, vb_ref = outs
    acc = jnp.dot(x_ref[...].astype(BF16), w_ref[...], preferred_element_type=F32)
    q_ref[...] = (acc[:, :wa] * scale).astype(BF16)
    k = acc[:, wa:2 * wa]
    v = acc[:, 2 * wa:3 * wa]
    k_ref[...] = k
    v_ref[...] = v
    kb_ref[...] = k.astype(BF16)
    vb_ref[...] = v.astype(BF16)


def _proj_mix_body(rows, consts, outs, *, wb, wc):
    (x_ref,), (w_ref,) = rows, consts
    lx_ref, glg_ref, glu_ref = outs
    acc = jnp.dot(x_ref[...].astype(BF16), w_ref[...], preferred_element_type=F32)
    lx_ref[...] = acc[:, :wb]
    glg_ref[...] = jax.nn.gelu(acc[:, wb:2 * wb])
    glu_ref[...] = acc[:, 2 * wb:2 * wb + wc] * _sigmoid(acc[:, 2 * wb + wc:])


def _sb_prompt_kernel(bias_ref, q_ref, k_ref, v_ref, tri_ref, o_init, o_ref, *, tq, tk, dh, heads_per_step):
    del o_init
    hg = pl.program_id(1)
    qi = pl.program_id(2)
    pair_w = 2 * dh
    lane = lax.broadcasted_iota(jnp.int32, (1, pair_w), 1)
    tri = tri_ref[...]
    col_minus_row = lax.broadcasted_iota(jnp.int32, (tq, tk), 1) - lax.broadcasted_iota(jnp.int32, (tq, tk), 0)
    q_start = qi * tq
    qms, biases, cols = [], [], []
    for hh in range(heads_per_step):
        c = slice((hh // 2) * pair_w, (hh // 2 + 1) * pair_w)
        q2 = (q_ref[:, c].astype(F32) * LOG2E).astype(BF16)
        lo = dh * (hh % 2)
        qms.append(jnp.where((lane >= lo) & (lane < lo + dh), q2, jnp.zeros_like(q2)))
        biases.append(bias_ref[heads_per_step * hg + hh] * LOG2E)
        cols.append(c)

    def tiles(start, carry, masked):
        out = []
        if masked:
            causal = col_minus_row < q_start - start
        for hh in range(heads_per_step):
            run, acc = carry[hh]
            kb = k_ref[pl.ds(start, tk), cols[hh]]
            vb = v_ref[pl.ds(start, tk), cols[hh]]
            y = lax.dot_general(qms[hh], kb, (((1,), (1,)), ((), ())), preferred_element_type=F32) + biases[hh]
            ls = jnp.minimum(y, 0.0) - jnp.log2(1.0 + jnp.exp2(_neg_abs(y)))
            lr = ls - y
            if masked:
                lr = jnp.where(causal, lr, 0.0)
            aft = jnp.dot(lr.astype(BF16), tri, preferred_element_type=F32)
            w = jnp.exp2(ls + (aft + run))
            if masked:
                w = jnp.where(causal, w, 0.0)
            acc = acc + jnp.dot(w.astype(BF16), vb, preferred_element_type=F32)
            run = run + aft[:, 0:1] + lr[:, 0:1]
            out.append((run, acc))
        return tuple(out)

    carry = tuple((jnp.zeros((tq, 1), F32), jnp.zeros((tq, pair_w), F32)) for _ in range(heads_per_step))
    per_q = tq // tk
    for d in range(per_q):
        carry = tiles(pl.multiple_of(q_start + (per_q - 1 - d) * tk, tk), carry, True)
    n_before = qi * per_q
    carry = lax.fori_loop(0, n_before, lambda jj, c: tiles(pl.multiple_of((n_before - 1 - jj) * tk, tk), c, False),
                          carry)
    for p in range(heads_per_step // 2):
        o_ref[:, cols[2 * p]] = jnp.where(lane < dh, carry[2 * p][1], carry[2 * p + 1][1]).astype(o_ref.dtype)


def _strict_lower_ones(n, dtype):
    s = lax.broadcasted_iota(jnp.int32, (n, n), 0)
    j = lax.broadcasted_iota(jnp.int32, (n, n), 1)
    return (s > j).astype(dtype)


def _sb_prompt(qs, kb, vb, sb_bias, o_init, *, batch, seq, heads, dh, tq, tk):
    nt, wa = qs.shape
    nq = seq // tq
    tri = _strict_lower_ones(tk, BF16)
    hps = SB_HEADS_PER_STEP
    grid_spec = pltpu.PrefetchScalarGridSpec(
        num_scalar_prefetch=1, grid=(batch, heads // hps, nq),
        in_specs=[
            pl.BlockSpec((tq, hps * dh), lambda b, h, i, *_: (b * nq + i, h)),
            pl.BlockSpec((seq, hps * dh), lambda b, h, i, *_: (b, h)),
            pl.BlockSpec((seq, hps * dh), lambda b, h, i, *_: (b, h)),
            pl.BlockSpec(tri.shape, lambda b, h, i, *_: (0, 0)),
            pl.BlockSpec(memory_space=pl.ANY),
        ],
        out_specs=pl.BlockSpec((tq, hps * dh), lambda b, h, i, *_: (b * nq + i, h)),
    )
    return pl.pallas_call(
        functools.partial(_sb_prompt_kernel, tq=tq, tk=tk, dh=dh, heads_per_step=hps), grid_spec=grid_spec,
        out_shape=jax.ShapeDtypeStruct((nt, wa), BF16), input_output_aliases={5: 0},
        compiler_params=_params(("parallel", "parallel", "arbitrary")), name="sb_prompt")(
            sb_bias, qs, kb, vb, tri, o_init)


def _sb_sample_kernel(pt_ref, bias_ref, qm_ref, kn_ref, vn_ref, hm_ref, tri_ref, *refs,
                      pages_per_step, past):
    k_refs = refs[:pages_per_step]
    v_refs = refs[pages_per_step:2 * pages_per_step]
    o_ref, run_ref, acc_ref = refs[2 * pages_per_step:]
    del pt_ref
    b = pl.program_id(0)
    g = pl.program_id(1)
    hrows = qm_ref.shape[1]
    bias = bias_ref[...]
    tri = tri_ref[...]
    qm = qm_ref[0]
    nt_dims = (((1,), (1,)), ((), ()))

    @pl.when(g == 0)
    def _():
        z = jnp.sum(qm.astype(F32) * kn_ref[0], axis=-1, keepdims=True) + bias
        visible = jnp.full(z.shape, past, jnp.int32) < jnp.full(z.shape, past, jnp.int32)
        lr = jnp.where(visible, _neg_softplus(z), 0.0)
        w = jnp.where(visible, jnp.exp(z + lr), 0.0)
        run_ref[...] = lr
        acc_ref[...] = w * vn_ref[0]

    zs = [jnp.dot(qm, k_refs[s][0, 0].astype(BF16), preferred_element_type=F32) + bias
          for s in range(pages_per_step)]
    z = jnp.concatenate(zs, axis=0)
    lr = _neg_softplus(z)
    hi, lo = _split_bf16(lr)
    aft = jnp.dot(hi, tri, preferred_element_type=F32) + jnp.dot(lo, tri, preferred_element_type=F32)
    total = aft[:, 0:1] + lr[:, 0:1]
    run = run_ref[...]
    runs = []
    for s in range(pages_per_step):
        runs.append(run)
        run = run + total[s * hrows:(s + 1) * hrows]
    run_ref[...] = run
    w = jnp.exp(z + lr + aft + jnp.concatenate(runs, axis=0)).astype(BF16)
    acc = acc_ref[...]
    for s in range(pages_per_step):
        acc = acc + lax.dot_general(w[s * hrows:(s + 1) * hrows], v_refs[s][0, 0].astype(BF16), nt_dims,
                                    preferred_element_type=F32)
    acc_ref[...] = acc

    @pl.when(g == pl.num_programs(1) - 1)
    def _():
        o_ref[pl.ds(b, 1), :] = jnp.sum(acc * hm_ref[...], axis=0, keepdims=True)


def _sb_sample(q, k_new, v_new, cache_k, cache_v, page_table, sb_bias, *, layer, heads, dh, pages_per_step):
    s_rows, wa = q.shape
    n_pages = page_table.shape[1]
    page = cache_k.shape[2]
    n_phys = cache_k.shape[1]
    ck = jnp.transpose(cache_k, (0, 1, 3, 4, 2)).reshape(cache_k.shape[0], n_phys, wa, page)
    cv = jnp.transpose(cache_v, (0, 1, 3, 4, 2)).reshape(cache_v.shape[0], n_phys, wa, page)
    hrows = 2 * heads
    hsel = (lax.broadcasted_iota(jnp.int32, (hrows, wa), 1) // dh
            == lax.broadcasted_iota(jnp.int32, (hrows, wa), 0))
    qm = jnp.where(hsel[None], q[:, None, :], 0.0).astype(BF16)
    bias = jnp.concatenate([sb_bias, jnp.zeros((hrows - heads,), F32)])[:, None]
    tri = _strict_lower_ones(page, BF16)
    steps = n_pages // pages_per_step

    def page_map(s):
        def index_map(b, g, pt):
            return (layer, pt[b, n_pages - 1 - (g * pages_per_step + s)], 0, 0)
        return index_map

    page_specs = [pl.BlockSpec((1, 1, wa, page), page_map(s)) for s in range(pages_per_step)]
    grid_spec = pltpu.PrefetchScalarGridSpec(
        num_scalar_prefetch=1, grid=(s_rows, steps),
        in_specs=[
            pl.BlockSpec((hrows, 1), lambda b, g, pt: (0, 0)),
            pl.BlockSpec((1, hrows, wa), lambda b, g, pt: (b, 0, 0)),
            pl.BlockSpec((1, 1, wa), lambda b, g, pt: (b, 0, 0)),
            pl.BlockSpec((1, 1, wa), lambda b, g, pt: (b, 0, 0)),
            pl.BlockSpec((hrows, wa), lambda b, g, pt: (0, 0)),
            pl.BlockSpec((page, page), lambda b, g, pt: (0, 0)),
        ] + page_specs + page_specs,
        out_specs=pl.BlockSpec((s_rows, wa), lambda b, g, pt: (0, 0)),
        scratch_shapes=[pltpu.VMEM((hrows, 1), F32), pltpu.VMEM((hrows, wa), F32)],
    )
    return pl.pallas_call(
        functools.partial(_sb_sample_kernel, pages_per_step=pages_per_step, past=n_pages * page),
        grid_spec=grid_spec, out_shape=jax.ShapeDtypeStruct((s_rows, wa), F32),
        compiler_params=_params(("arbitrary", "arbitrary")), name="sb_sample")(
            page_table, bias, qm, k_new[:, None, :], v_new[:, None, :], hsel.astype(F32), tri,
            *([ck] * pages_per_step), *([cv] * pages_per_step))


def _lru_gates(xc, wa_ref, ba_ref, wx_ref, bx_ref, lam_ref):
    xb = xc.astype(BF16)
    r = _sigmoid(jnp.dot(xb, wa_ref[...], preferred_element_type=F32) + ba_ref[...])
    i = _sigmoid(jnp.dot(xb, wx_ref[...], preferred_element_type=F32) + bx_ref[...])
    log_a = LRU_C * r * _neg_softplus(-lam_ref[...])
    a = jnp.exp(log_a)
    u = jnp.sqrt(-jnp.tanh(log_a) * (a * a + 1.0)) * (i * xc)
    return a, u


def _lru_prompt_kernel(lx_ref, glg_ref, cw_ref, cb_ref, wa_ref, ba_ref, wx_ref, bx_ref, lam_ref, ob_init,
                       ob_ref, h_ref, buf_ref, xpad, a_s, u_s, *, seq, chunk, taps):
    del ob_init
    width = lx_ref.shape[1]
    xpad[0:SUBLANES, :] = jnp.zeros((SUBLANES, width), F32)
    xpad[SUBLANES:, :] = lx_ref[...]
    rowid = lax.broadcasted_iota(jnp.int32, (SUBLANES, width), 0)
    h = jnp.zeros((1, width), F32)
    for c in range(seq // chunk):
        base = SUBLANES + c * chunk
        xc = cb_ref[...] + cw_ref[taps - 1:taps, :] * xpad[base:base + chunk, :]
        for d in range(1, taps):
            xc = xc + cw_ref[taps - 1 - d:taps - d, :] * xpad[base - d:base - d + chunk, :]
        a, u = _lru_gates(xc, wa_ref, ba_ref, wx_ref, bx_ref, lam_ref)
        a_s[...] = a
        u_s[...] = u

        def group(gi, h):
            r0 = pl.multiple_of(gi * SUBLANES, SUBLANES)
            ga = a_s[pl.ds(r0, SUBLANES), :]
            gu = u_s[pl.ds(r0, SUBLANES), :]
            for d in (1, 2, 4):
                keep = rowid >= d
                gu = jnp.where(keep, ga * pltpu.roll(gu, d, 0) + gu, gu)
                ga = jnp.where(keep, ga * pltpu.roll(ga, d, 0), ga)
            hs = ga * h + gu
            u_s[pl.ds(r0, SUBLANES), :] = hs
            return hs[SUBLANES - 1:SUBLANES, :]

        h = lax.fori_loop(0, chunk // SUBLANES, group, h)
        rows = slice(c * chunk, (c + 1) * chunk)
        ob_ref[rows, :] = (u_s[...] * glg_ref[rows, :].astype(F32)).astype(ob_ref.dtype)
    h_ref[0] = h
    buf_ref[0] = lx_ref[seq - (taps - 1):seq, :]


def _lru_prompt(lx, glg, cw, cb, wa_bd, ba, wx_bd, bx, lam, ob_init, *, batch, seq, chunk):
    nt, width = lx.shape
    taps = cw.shape[0]
    consts = [cw, cb, wa_bd, ba, wx_bd, bx, lam]
    return pl.pallas_call(
        functools.partial(_lru_prompt_kernel, seq=seq, chunk=chunk, taps=taps),
        grid=(batch,),
        in_specs=[pl.BlockSpec((seq, width), lambda b: (b, 0)),
                  pl.BlockSpec((seq, width), lambda b: (b, 0))] + [_full_spec(a) for a in consts]
        + [pl.BlockSpec(memory_space=pl.ANY)],
        input_output_aliases={2 + len(consts): 0},
        out_specs=[pl.BlockSpec((seq, width), lambda b: (b, 0)),
                   pl.BlockSpec((1, 1, width), lambda b: (b, 0, 0)),
                   pl.BlockSpec((1, taps - 1, width), lambda b: (b, 0, 0))],
        out_shape=[jax.ShapeDtypeStruct((nt, width), BF16),
                   jax.ShapeDtypeStruct((batch, 1, width), F32),
                   jax.ShapeDtypeStruct((batch, taps - 1, width), F32)],
        scratch_shapes=[pltpu.VMEM((seq + SUBLANES, width), F32),
                        pltpu.VMEM((chunk, width), F32), pltpu.VMEM((chunk, width), F32)],
        compiler_params=_params(("parallel",)), name="lru_prompt")(lx, glg, *consts, ob_init)


def _conf_prompt_kernel(glu_ref, cw_ref, cb_ref, g_ref, b_ref, oc_init, oc_ref, buf_ref, xpad, win, part_s, cc_s,
                        *, seq, chunk, taps, pad):
    del oc_init
    width = glu_ref.shape[1]
    xpad[0:pad, :] = jnp.zeros((pad, width), F32)
    xpad[pad:pad + seq, :] = glu_ref[...]
    xpad[pad + seq:, :] = jnp.zeros((SUBLANES, width), F32)
    lead = pad - (taps - 1)

    def one_chunk(c, carry):
        t0 = pl.multiple_of(c * chunk, chunk)
        win[...] = xpad[pl.ds(t0, chunk + pad + SUBLANES), :]
        for l0 in range(0, width, LANES):
            lanes = slice(l0, l0 + LANES)
            acc = cb_ref[:, lanes]
            for rem in range(SUBLANES):
                part = None
                for j in range(taps):
                    if (lead + j) % SUBLANES == rem:
                        a0 = lead + j - rem
                        term = cw_ref[j:j + 1, lanes] * win[a0:a0 + chunk + SUBLANES, lanes]
                        part = term if part is None else part + term
                if part is not None:
                    part_s[...] = part
                    acc = acc + part_s[rem:rem + chunk, :]
            cc_s[:, lanes] = acc
        y = _layer_norm(cc_s[...], g_ref[...], b_ref[...])
        oc_ref[pl.ds(t0, chunk), :] = (y * _sigmoid(y)).astype(oc_ref.dtype)
        return carry

    lax.fori_loop(0, seq // chunk, one_chunk, 0)
    buf_ref[0] = glu_ref[seq - (taps - 1):seq, :]


def _conf_prompt(glu, cw, cb, g, b, oc_init, *, batch, seq, chunk):
    nt, width = glu.shape
    taps = cw.shape[0]
    pad = -(-(taps - 1) // SUBLANES) * SUBLANES
    consts = [cw, cb, g, b]
    return pl.pallas_call(
        functools.partial(_conf_prompt_kernel, seq=seq, chunk=chunk, taps=taps, pad=pad),
        grid=(batch,),
        in_specs=[pl.BlockSpec((seq, width), lambda b: (b, 0))] + [_full_spec(a) for a in consts]
        + [pl.BlockSpec(memory_space=pl.ANY)],
        input_output_aliases={1 + len(consts): 0},
        out_specs=[pl.BlockSpec((seq, width), lambda b: (b, 0)),
                   pl.BlockSpec((1, taps - 1, width), lambda b: (b, 0, 0))],
        out_shape=[jax.ShapeDtypeStruct((nt, width), BF16),
                   jax.ShapeDtypeStruct((batch, taps - 1, width), F32)],
        scratch_shapes=[pltpu.VMEM((seq + pad + SUBLANES, width), F32),
                        pltpu.VMEM((chunk + pad + SUBLANES, width), F32),
                        pltpu.VMEM((chunk + SUBLANES, LANES), F32), pltpu.VMEM((chunk, width), F32)],
        compiler_params=_params(("parallel",)), name="conf_prompt")(glu, *consts, oc_init)


def _sample_mix_kernel(lx_ref, glg_ref, glu_ref, oas_ref, h0_ref, lbuf_ref, cbuf_ref,
                       lcw_ref, lcb_ref, wa_ref, ba_ref, wx_ref, bx_ref, lam_ref,
                       ccw_ref, ccb_ref, cg_ref, cbeta_ref, oa_in, ob_in, oc_in,
                       oa_ref, ob_ref, oc_ref, h_ref, lbuf_out, cbuf_out, *, ltaps, ctaps):
    del oa_in, ob_in, oc_in
    oa_ref[...] = oas_ref[...].astype(oa_ref.dtype)
    lx = lx_ref[...]
    xc = lcb_ref[...] + lcw_ref[ltaps - 1:ltaps, :] * lx
    for j in range(ltaps - 1):
        xc = xc + lcw_ref[j:j + 1, :] * lbuf_ref[j]
    a, u = _lru_gates(xc, wa_ref, ba_ref, wx_ref, bx_ref, lam_ref)
    h = a * h0_ref[...] + u
    h_ref[...] = h
    ob_ref[...] = (h * glg_ref[...].astype(F32)).astype(ob_ref.dtype)
    for j in range(ltaps - 2):
        lbuf_out[j] = lbuf_ref[j + 1]
    lbuf_out[ltaps - 2] = lx

    glu = glu_ref[...]
    cc = ccb_ref[...] + ccw_ref[ctaps - 1:ctaps, :] * glu
    for j in range(ctaps - 1):
        cc = cc + ccw_ref[j:j + 1, :] * cbuf_ref[j]
    y = _layer_norm(cc, cg_ref[...], cbeta_ref[...])
    oc_ref[...] = (y * _sigmoid(y)).astype(oc_ref.dtype)
    for j in range(ctaps - 2):
        cbuf_out[j] = cbuf_ref[j + 1]
    cbuf_out[ctaps - 2] = glu


def _sample_mix(lx, glg, glu, oa_s, h0, lbuf, cbuf, lru_consts, conf_consts, oa_all, ob_all, oc_all, *, row0, rows):
    width = lx.shape[1]
    blk = row0 // rows
    ltaps = lru_consts[0].shape[0]
    ctaps = conf_consts[0].shape[0]
    lbuf_t = jnp.transpose(lbuf, (1, 0, 2))
    cbuf_t = jnp.transpose(cbuf, (1, 0, 2))
    consts = list(lru_consts) + list(conf_consts)
    tail = pl.BlockSpec((rows, width), lambda i: (blk, 0))
    any_spec = pl.BlockSpec(memory_space=pl.ANY)
    ins = [lx, glg, glu, oa_s, h0, lbuf_t, cbuf_t] + consts + [oa_all, ob_all, oc_all]
    in_specs = ([tail, tail, tail, _full_spec(oa_s), _full_spec(h0), _full_spec(lbuf_t), _full_spec(cbuf_t)]
                + [_full_spec(a) for a in consts] + [any_spec, any_spec, any_spec])
    n_in = len(ins)
    oa, ob, oc, h, lb, cb = pl.pallas_call(
        functools.partial(_sample_mix_kernel, ltaps=ltaps, ctaps=ctaps),
        grid=(1,), in_specs=in_specs,
        out_specs=[tail, tail, tail, _full_spec(h0), _full_spec(lbuf_t), _full_spec(cbuf_t)],
        out_shape=[jax.ShapeDtypeStruct(a.shape, a.dtype) for a in (oa_all, ob_all, oc_all)]
        + [jax.ShapeDtypeStruct(h0.shape, F32), jax.ShapeDtypeStruct(lbuf_t.shape, F32),
           jax.ShapeDtypeStruct(cbuf_t.shape, F32)],
        input_output_aliases={n_in - 3: 0, n_in - 2: 1, n_in - 1: 2},
        compiler_params=_params(("arbitrary",)), name="sample_mix")(*ins)
    return oa, ob, oc, h, jnp.transpose(lb, (1, 0, 2)), jnp.transpose(cb, (1, 0, 2))


def _merge_body(rows, consts, outs, *, alpha, n_experts, d_model):
    x_ref, oa_ref, ob_ref, oc_ref = rows
    wgate_ref, bgate_ref, wbr_ref, wout_ref, g_ref, b_ref, rw_ref, rb_ref = consts
    x1_ref, ridx_ref, rgw_ref = outs
    x = x_ref[...]
    xb = x.astype(BF16)
    m = None
    for i, o_ref in enumerate((oa_ref, ob_ref, oc_ref)):
        cols = slice(i * d_model, (i + 1) * d_model)
        gate = _sigmoid(jnp.dot(xb, wgate_ref[:, cols], preferred_element_type=F32) + bgate_ref[:, cols])
        p = gate * jnp.dot(o_ref[...], wbr_ref[i], preferred_element_type=F32)
        m = p if m is None else m + p
    y = alpha * x + jnp.dot(m.astype(BF16), wout_ref[...], preferred_element_type=F32)
    x1 = _layer_norm(y, g_ref[...], b_ref[...])
    x1_ref[...] = x1
    logits = jnp.dot(x1.astype(BF16), rw_ref[...], preferred_element_type=F32) + rb_ref[...]
    lane = lax.broadcasted_iota(jnp.int32, logits.shape, 1)
    lanef = lane.astype(F32)
    logits = jnp.where(lane < n_experts, logits, NEG_INF)
    idx_out = jnp.zeros(logits.shape, F32)
    val_out = jnp.zeros(logits.shape, F32)
    top = None
    for k in range(TOP_K):
        mx = jnp.max(logits, axis=-1, keepdims=True)
        sel = jnp.min(jnp.where(logits == mx, lanef, float(LANES)), axis=-1, keepdims=True)
        top = mx if top is None else top
        idx_out = jnp.where(lane == k, sel, idx_out)
        val_out = jnp.where(lane == k, jnp.exp(mx - top), val_out)
        logits = jnp.where(lanef == sel, NEG_INF, logits)
    ridx_ref[...] = idx_out.astype(jnp.int32)
    rgw_ref[...] = val_out / jnp.sum(val_out, axis=-1, keepdims=True)


def _for_each_row(tt, fn):
    def body(g, c):
        for u in range(ROW_UNROLL):
            fn(g * ROW_UNROLL + u)
        return c

    lax.fori_loop(0, tt // ROW_UNROLL, body, 0)


def _dispatch_kernel(dest_hbm, x_ref, xs_in, xs_hbm, dest_s, sem_i, sem_r, *, tt):
    del xs_in
    i = pl.program_id(0)
    cp = pltpu.make_async_copy(dest_hbm.at[i], dest_s, sem_i)
    cp.start()
    cp.wait()

    def copies(r):
        src = x_ref.at[pl.ds(r, 1), :]
        return [pltpu.make_async_copy(src, xs_hbm.at[pl.ds(dest_s[0, r * TOP_K + k], 1), :], sem_r)
                for k in range(TOP_K)]

    _for_each_row(tt, lambda r: [c.start(priority=k % 2) for k, c in enumerate(copies(r))])
    _for_each_row(tt, lambda r: [c.wait() for c in copies(r)])


def _dispatch(x1, dest2, xs_zero, *, tt):
    nt, d = x1.shape
    any_spec = pl.BlockSpec(memory_space=pl.ANY)
    return pl.pallas_call(
        functools.partial(_dispatch_kernel, tt=tt), grid=(nt // tt,),
        in_specs=[any_spec, pl.BlockSpec((tt, d), lambda i: (i, 0)), any_spec], out_specs=any_spec,
        out_shape=jax.ShapeDtypeStruct(xs_zero.shape, xs_zero.dtype),
        scratch_shapes=[pltpu.SMEM((1, tt * TOP_K), jnp.int32), pltpu.SemaphoreType.DMA, pltpu.SemaphoreType.DMA],
        input_output_aliases={2: 0},
        compiler_params=_params(("arbitrary",)), name="moe_dispatch")(dest2, x1, xs_zero)


def _expert_kernel(be_ref, nu_ref, first_ref, x_ref, wg_ref, bg_ref, wu_ref, bu_ref, wd_ref, bd_ref, y_ref,
                   wg_s, wu_s, wd_s):
    i = pl.program_id(0)

    @pl.when(first_ref[i] == 1)
    def _():
        wg_s[...] = wg_ref[0, 0].astype(BF16)
        wu_s[...] = wu_ref[0, 0].astype(BF16)
        wd_s[...] = wd_ref[0, 0].astype(BF16)

    @pl.when(i < nu_ref[0])
    def _():
        xb = x_ref[...].astype(BF16)
        g = jnp.minimum(jnp.dot(xb, wg_s[...], preferred_element_type=F32) + bg_ref[0], SWIGLU_LIMIT)
        u = jnp.clip(jnp.dot(xb, wu_s[...], preferred_element_type=F32) + bu_ref[0], -SWIGLU_LIMIT, SWIGLU_LIMIT)
        h = (u + 1.0) * (g * _sigmoid(SWIGLU_ALPHA * g))
        y_ref[...] = jnp.dot(h.astype(BF16), wd_s[...], preferred_element_type=F32) + bd_ref[0]

    @pl.when(i >= nu_ref[0])
    def _():
        y_ref[...] = jnp.zeros(y_ref.shape, y_ref.dtype)


def _experts(xs, block_e, n_used, first, wg, bg, wu, bu, wd, bd, *, layer):
    rows, d = xs.shape
    dff = wg.shape[3]
    nb = rows // EXPERT_ROWS
    wspec = lambda shape: pl.BlockSpec((1, 1) + shape, lambda i, be, nu, fi: (layer, be[i], 0, 0))
    bspec = lambda n: pl.BlockSpec((1, 1, n), lambda i, be, nu, fi: (be[i], 0, 0))
    grid_spec = pltpu.PrefetchScalarGridSpec(
        num_scalar_prefetch=3, grid=(nb,),
        in_specs=[pl.BlockSpec((EXPERT_ROWS, d), lambda i, be, nu, fi: (i, 0)),
                  wspec((d, dff)), bspec(dff), wspec((d, dff)), bspec(dff), wspec((dff, d)), bspec(d)],
        out_specs=pl.BlockSpec((EXPERT_ROWS, d), lambda i, be, nu, fi: (i, 0)),
        scratch_shapes=[pltpu.VMEM((d, dff), BF16), pltpu.VMEM((d, dff), BF16), pltpu.VMEM((dff, d), BF16)])
    return pl.pallas_call(
        _expert_kernel, grid_spec=grid_spec, out_shape=jax.ShapeDtypeStruct((rows, d), F32),
        compiler_params=_params(("arbitrary",)), name="moe_experts")(
            block_e, n_used, first, xs, wg, bg[:, None, :], wu, bu[:, None, :], wd, bd[:, None, :])


def _combine_kernel(dest_hbm, ys_hbm, x1_ref, gw_ref, g_ref, b_ref, x2_ref, dest_s, ybuf, sem_i, sem_r, *, tt, alpha):
    i = pl.program_id(0)
    slot = i % 2

    def gathers(r, s):
        return [pltpu.make_async_copy(ys_hbm.at[pl.ds(dest_s[s, 0, r * TOP_K + k], 1), :],
                                      ybuf.at[s, k, pl.ds(r, 1), :], sem_r.at[s]) for k in range(TOP_K)]

    def fetch(tile, s):
        cp = pltpu.make_async_copy(dest_hbm.at[tile], dest_s.at[s], sem_i)
        cp.start()
        cp.wait()
        _for_each_row(tt, lambda r: [c.start(priority=k % 2) for k, c in enumerate(gathers(r, s))])

    @pl.when(i == 0)
    def _():
        fetch(0, 0)

    for s in range(2):
        @pl.when(slot == s)
        def _(s=s):
            @pl.when(i + 1 < pl.num_programs(0))
            def _():
                fetch(i + 1, 1 - s)

            _for_each_row(tt, lambda r: [c.wait() for c in gathers(r, s)])
            gw = gw_ref[...]
            moe = gw[:, 0:1] * ybuf[s, 0]
            for k in range(1, TOP_K):
                moe = moe + gw[:, k:k + 1] * ybuf[s, k]
            x2_ref[...] = _layer_norm(alpha * x1_ref[...] + moe, g_ref[...], b_ref[...])


def _combine(ys, dest2, x1, rgw, g, b, *, tt, alpha):
    nt, d = x1.shape
    return pl.pallas_call(
        functools.partial(_combine_kernel, tt=tt, alpha=alpha), grid=(nt // tt,),
        in_specs=[pl.BlockSpec(memory_space=pl.ANY), pl.BlockSpec(memory_space=pl.ANY),
                  pl.BlockSpec((tt, d), lambda i: (i, 0)), pl.BlockSpec((tt, LANES), lambda i: (i, 0)),
                  _full_spec(g), _full_spec(b)],
        out_specs=pl.BlockSpec((tt, d), lambda i: (i, 0)),
        out_shape=jax.ShapeDtypeStruct((nt, d), F32),
        scratch_shapes=[pltpu.SMEM((2, 1, tt * TOP_K), jnp.int32), pltpu.VMEM((2, TOP_K, tt, d), F32),
                        pltpu.SemaphoreType.DMA, pltpu.SemaphoreType.DMA((2,))],
        compiler_params=_params(("arbitrary",)), name="moe_combine")(dest2, ys, x1, rgw, g, b)


def _route_rank_kernel(ridx_ref, tri_ref, rank_ref, cnt_ref, carry):
    @pl.when(pl.program_id(0) == 0)
    def _():
        carry[...] = jnp.zeros(carry.shape, F32)

    idx = ridx_ref[...]
    lane = lax.broadcasted_iota(jnp.int32, idx.shape, 1)
    onehot = jnp.zeros(idx.shape, F32)
    for k in range(TOP_K):
        onehot = onehot + (idx[:, k:k + 1] == lane).astype(F32)
    rank_ref[...] = jnp.dot(tri_ref[...], onehot.astype(BF16), preferred_element_type=F32) + carry[...]
    carry[...] = carry[...] + jnp.sum(onehot, axis=0, keepdims=True)
    cnt_ref[...] = carry[...]


def _route(ridx, n_experts, tt):
    nt = ridx.shape[0]
    n_assign = nt * TOP_K
    tri = _strict_lower_ones(tt, BF16)
    rank, cnt = pl.pallas_call(
        _route_rank_kernel, grid=(nt // tt,),
        in_specs=[pl.BlockSpec((tt, LANES), lambda i: (i, 0)), _full_spec(tri)],
        out_specs=[pl.BlockSpec((tt, LANES), lambda i: (i, 0)), pl.BlockSpec((1, LANES), lambda i: (0, 0))],
        out_shape=[jax.ShapeDtypeStruct((nt, LANES), F32), jax.ShapeDtypeStruct((1, LANES), F32)],
        scratch_shapes=[pltpu.VMEM((1, LANES), F32)],
        compiler_params=_params(("arbitrary",)), name="route_rank")(ridx, tri)
    counts = cnt[0, :n_experts].astype(jnp.int32)
    padded = (counts + EXPERT_ROWS - 1) // EXPERT_ROWS * EXPERT_ROWS
    pend = jnp.cumsum(padded)
    pstart = pend - padded
    slot = rank[:, :n_experts].astype(jnp.int32) + pstart[None, :]
    dest = jnp.take_along_axis(slot, ridx[:, :TOP_K], axis=1)
    n_blocks = -(-n_assign // EXPERT_ROWS) + n_experts
    block_row0 = jnp.arange(n_blocks, dtype=jnp.int32) * EXPERT_ROWS
    block_e = jnp.minimum(jnp.sum((pend[None, :] <= block_row0[:, None]).astype(jnp.int32), axis=1), n_experts - 1)
    first = jnp.concatenate([jnp.ones((1,), jnp.int32), (block_e[1:] != block_e[:-1]).astype(jnp.int32)])
    n_used = (pend[-1:] // EXPERT_ROWS).astype(jnp.int32)
    return dest.reshape(nt // tt, 1, tt * TOP_K), block_e, n_used, first, n_blocks


def _block_diag(w):
    nb, bi, bo = w.shape
    eye = jnp.eye(nb, dtype=w.dtype)
    return (w[:, :, None, :] * eye[:, None, :, None]).reshape(nb * bi, nb * bo)


def _largest_tile(n, limit, align):
    best = align
    for t in range(align, limit + 1, align):
        if n % t == 0:
            best = t
    return best


def kernel(x_prompt, x_sample, cache_k, cache_v, state_lru_h, state_lru_conv, state_conf_conv, page_table, w_in, b_merge, sb_bias, lru_conv_w, lru_conv_b, lru_wa, lru_ba, lru_wx, lru_bx, lru_lambda, conf_conv_w, conf_conv_b, conf_ln_g, conf_ln_b, w_branch, w_out, ln1_g, ln1_b, router_w, router_b, moe_w_gate, moe_b_gate, moe_w_up, moe_b_up, moe_w_down, moe_b_down, ln2_g, ln2_b):
    batch, seq, d_model = x_prompt.shape
    s_rows = x_sample.shape[0]
    depth = w_in.shape[0]
    heads, dh = cache_k.shape[3], cache_k.shape[4]
    wa = heads * dh
    wb = lru_conv_w.shape[2]
    wc = conf_conv_w.shape[2]
    n_experts = router_w.shape[2]
    n_prompt = batch * seq
    nt = n_prompt + s_rows
    alpha = float((2 * depth) ** 0.25)
    tm = _largest_tile(nt, 640, 16)
    tt = _largest_tile(nt, 320, 8)
    row = lambda v: v[None, :]

    x = jnp.concatenate([x_prompt.reshape(n_prompt, d_model), x_sample.reshape(s_rows, d_model)], axis=0)
    outs = {k: [] for k in ("kp", "vp", "ks", "vs", "hp", "hs", "lp", "ls", "cp", "cs")}
    for l in range(depth):
        w_l = w_in[l].astype(BF16)
        c0, c1, c2 = 3 * wa, 3 * wa + 2 * wb + 2 * wc, w_in.shape[2]
        qs, k, v, kb, vb = _tokenwise_call(
            functools.partial(_proj_qkv_body, wa=wa, scale=dh ** -0.5), nt, tm, [x], [w_l[:, :c0]],
            [(wa, BF16), (wa, F32), (wa, F32), (wa, BF16), (wa, BF16)], "proj_qkv")
        lx, glg, glu = _tokenwise_call(
            functools.partial(_proj_mix_body, wb=wb, wc=wc), nt, tm, [x], [w_l[:, c0:c1]],
            [(wb, F32), (wb, F32), (wc, F32)], "proj_mix")

        o_a = _sb_prompt(qs, kb, vb, sb_bias[l], jnp.zeros((nt, wa), BF16), batch=batch, seq=seq, heads=heads,
                         dh=dh, tq=512, tk=512)
        q_s = qs[n_prompt:].astype(F32)
        oa_s = _sb_sample(q_s, k[n_prompt:], v[n_prompt:], cache_k, cache_v, page_table, sb_bias[l],
                          layer=l, heads=heads, dh=dh, pages_per_step=16)

        lru_consts = [lru_conv_w[l], row(lru_conv_b[l]), _block_diag(lru_wa[l]).astype(BF16), row(lru_ba[l]),
                      _block_diag(lru_wx[l]).astype(BF16), row(lru_bx[l]), row(lru_lambda[l])]
        conf_consts = [conf_conv_w[l], row(conf_conv_b[l]), row(conf_ln_g[l]), row(conf_ln_b[l])]
        o_b, h_p, lbuf_p = _lru_prompt(lx, glg, *lru_consts, jnp.zeros((nt, wb), BF16), batch=batch, seq=seq,
                                       chunk=256)
        o_c, cbuf_p = _conf_prompt(glu, *conf_consts, jnp.zeros((nt, wc), BF16), batch=batch, seq=seq, chunk=128)
        o_a, o_b, o_c, h_s, lbuf_s, cbuf_s = _sample_mix(
            lx, glg, glu, oa_s, state_lru_h[l], state_lru_conv[l], state_conf_conv[l], lru_consts, conf_consts,
            o_a, o_b, o_c, row0=n_prompt, rows=s_rows)

        rw = jnp.zeros((d_model, LANES), BF16).at[:, :n_experts].set(router_w[l].astype(BF16))
        rb = jnp.zeros((1, LANES), F32).at[0, :n_experts].set(router_b[l])
        x1, ridx, rgw = _tokenwise_call(
            functools.partial(_merge_body, alpha=alpha, n_experts=n_experts, d_model=d_model), nt, tm,
            [x, o_a, o_b, o_c],
            [w_l[:, c1:c2], b_merge[l].reshape(1, c2 - c1), w_branch[l].astype(BF16), w_out[l].astype(BF16),
             row(ln1_g[l]), row(ln1_b[l]), rw, rb],
            [(d_model, F32), (LANES, jnp.int32), (LANES, F32)], "merge")

        dest2, block_e, n_used, first, n_blocks = _route(ridx, n_experts, tt)
        xs = _dispatch(x1, dest2, jnp.zeros((n_blocks * EXPERT_ROWS, d_model), F32), tt=tt)
        ys = _experts(xs, block_e, n_used, first, moe_w_gate, moe_b_gate[l], moe_w_up, moe_b_up[l],
                      moe_w_down, moe_b_down[l], layer=l)
        x = _combine(ys, dest2, x1, rgw, row(ln2_g[l]), row(ln2_b[l]), tt=tt, alpha=alpha)

        outs["kp"].append(k[:n_prompt].reshape(batch, seq, heads, dh))
        outs["vp"].append(v[:n_prompt].reshape(batch, seq, heads, dh))
        outs["ks"].append(k[n_prompt:].reshape(s_rows, 1, heads, dh))
        outs["vs"].append(v[n_prompt:].reshape(s_rows, 1, heads, dh))
        outs["hp"].append(h_p.reshape(batch, wb))
        outs["hs"].append(h_s)
        outs["lp"].append(lbuf_p)
        outs["ls"].append(lbuf_s)
        outs["cp"].append(cbuf_p)
        outs["cs"].append(cbuf_s)

    st = {k_: jnp.stack(v_) for k_, v_ in outs.items()}
    return (x[:n_prompt].reshape(batch, seq, d_model), x[n_prompt:].reshape(s_rows, 1, d_model),
            st["kp"], st["vp"], st["ks"], st["vs"], st["hp"], st["hs"], st["lp"], st["ls"], st["cp"], st["cs"])
```
